```python
import math
import jax
import jax.numpy as jnp
from jax import lax
import numpy as np

D_MODEL = 1024
BATCH = 4
SEQ = 4096
DEPTH = 4
DEC_BATCH = 128
DEC_SEQ = 4
PAST_LEN = 8192
PAGE_SIZE = 128

N_META = 16
Q_BLOCK = 128
SB_HEADS = 8
SB_KV_HEADS = 2
SB_GROUP = SB_HEADS // SB_KV_HEADS
SB_HEAD_DIM = 64
SB_WIDTH = SB_HEADS * SB_HEAD_DIM
SB_KV_WIDTH = SB_KV_HEADS * SB_HEAD_DIM
LRU_WIDTH = 512
LRU_BLOCKS = 8
LRU_BLOCK_DIM = LRU_WIDTH // LRU_BLOCKS
CONV_WIDTH = 4
LRU_C = 8.0
MLA_HEADS = 8
Q_LORA = 256
KV_LORA = 128
QK_NOPE = 64
QK_ROPE = 32
V_HEAD = 64
MLA_WIDTH = MLA_HEADS * V_HEAD
MLA_SCALE = (QK_NOPE + QK_ROPE) ** -0.5
ROPE_THETA = 10000.0
N_BRANCH = 3
BRANCH_WIDTH = 512
IN_WIDTHS = (SB_WIDTH, SB_KV_WIDTH, SB_KV_WIDTH, LRU_WIDTH, LRU_WIDTH, Q_LORA, KV_LORA, QK_ROPE, N_BRANCH * D_MODEL)
D_IN = sum(IN_WIDTHS)
N_EXPERTS = 16
N_GROUPS = 4
EXPERTS_PER_GROUP = N_EXPERTS // N_GROUPS
TOP_K = 2
D_EXPERT = 256
DEEPNORM_ALPHA = (2 * DEPTH) ** 0.25
DEEPNORM_BETA = (8 * DEPTH) ** -0.25
LN_EPS = 1e-5
RMS_EPS = 1e-6
NEG_INF = -1e30

kernel_name = 'hybrid_sb_lru_mla_moe_step'


def layer_norm(x, g, b):
    xf = x.astype(jnp.float32)
    mu = jnp.mean(xf, axis=-1, keepdims=True)
    var = jnp.mean(jnp.square(xf - mu), axis=-1, keepdims=True)
    return ((xf - mu) * lax.rsqrt(var + LN_EPS) * g + b).astype(x.dtype)


def rms_norm(x, g):
    xf = x.astype(jnp.float32)
    return (xf * lax.rsqrt(jnp.mean(jnp.square(xf), axis=-1, keepdims=True) + RMS_EPS) * g).astype(x.dtype)


def apply_rope(x, pos):
    half = x.shape[-1] // 2
    inv = ROPE_THETA ** (-jnp.arange(half, dtype=jnp.float32) / half)
    ang = pos.astype(jnp.float32)[:, None] * inv[None, :]
    shape = (pos.shape[0],) + (1,) * (x.ndim - 3) + (half,)
    cos = jnp.cos(ang).reshape(shape)
    sin = jnp.sin(ang).reshape(shape)
    xf = x.astype(jnp.float32)
    x1, x2 = xf[..., :half], xf[..., half:]
    return jnp.concatenate([x1 * cos - x2 * sin, x2 * cos + x1 * sin], axis=-1).astype(x.dtype)


def split_in(x, w_in):
    h = jnp.einsum('bld,de->ble', x, w_in)
    return jnp.split(h, np.cumsum(IN_WIDTHS)[:-1].tolist(), axis=-1)


def split_ukv(w_ukv):
    w = w_ukv.reshape(KV_LORA, MLA_HEADS, QK_NOPE + V_HEAD)
    return w[..., :QK_NOPE], w[..., QK_NOPE:]


def blocked_queries(attend, q, q_pos):
    B, L = q.shape[:2]
    pad_front = (-N_META) % Q_BLOCK
    pad_back = (-(pad_front + L)) % Q_BLOCK
    qp = jnp.pad(q, [(0, 0), (pad_front, pad_back)] + [(0, 0)] * (q.ndim - 2))
    pp = jnp.pad(q_pos, (pad_front, pad_back), constant_values=-1)
    nb = qp.shape[1] // Q_BLOCK
    qb = jnp.moveaxis(qp.reshape((B, nb, Q_BLOCK) + q.shape[2:]), 1, 0)
    pb = pp.reshape(nb, Q_BLOCK)
    out = lax.map(lambda args: attend(args[0], args[1]), (qb, pb))
    out = jnp.moveaxis(out, 0, 1).reshape((B, nb * Q_BLOCK) + out.shape[3:])
    return out[:, pad_front:pad_front + L]


def sb_attend(q, k, v, q_pos, k_pos):
    B, Q = q.shape[:2]
    qg = q.reshape(B, Q, SB_KV_HEADS, SB_GROUP, SB_HEAD_DIM)
    z = jnp.einsum('bqkgd,bskd->bkgqs', qg, k).astype(jnp.float32) * (SB_HEAD_DIM ** -0.5)
    mask = k_pos[None, :] < q_pos[:, None]
    log_beta = jax.nn.log_sigmoid(z)
    log_keep = jnp.where(mask, jax.nn.log_sigmoid(-z), 0.0)
    suffix = lax.cumsum(log_keep, axis=log_keep.ndim - 1, reverse=True) - log_keep
    w = jnp.where(mask, jnp.exp(log_beta + suffix), 0.0)
    out = jnp.einsum('bkgqs,bskd->bqkgd', w.astype(v.dtype), v)
    return out.reshape(B, Q, SB_HEADS, SB_HEAD_DIM)


def softmax_attend(q, k, v, q_pos, k_pos):
    s = jnp.einsum('bqhd,bshd->bhqs', q, k).astype(jnp.float32) * MLA_SCALE
    s = jnp.where(k_pos[None, :] <= q_pos[:, None], s, NEG_INF)
    p = jax.nn.softmax(s, axis=-1)
    return jnp.einsum('bhqs,bshd->bqhd', p.astype(v.dtype), v)


def mla_latent_attend(q_nope, q_rope, c_all, kr_all, w_uk, w_uv, q_pos, k_pos):
    q_lat = jnp.einsum('bqhd,chd->bqhc', q_nope, w_uk)
    s = (jnp.einsum('bqhc,bsc->bhqs', q_lat, c_all) + jnp.einsum('bqhr,bsr->bhqs', q_rope, kr_all)).astype(jnp.float32) * MLA_SCALE
    s = jnp.where(k_pos[None, :] <= q_pos[:, None], s, NEG_INF)
    p = jax.nn.softmax(s, axis=-1)
    o_lat = jnp.einsum('bhqs,bsc->bqhc', p.astype(c_all.dtype), c_all)
    return jnp.einsum('bqhc,chd->bqhd', o_lat, w_uv)


def mla_project(dq, dkv, kr, pos, q_norm_g, w_uq, kv_norm_g):
    B, L = dq.shape[:2]
    q = jnp.einsum('blc,ce->ble', rms_norm(dq, q_norm_g), w_uq).reshape(B, L, MLA_HEADS, QK_NOPE + QK_ROPE)
    q_nope = q[..., :QK_NOPE]
    q_rope = apply_rope(q[..., QK_NOPE:], pos)
    ckv = rms_norm(dkv, kv_norm_g)
    k_rope = apply_rope(kr, pos)
    return q_nope, q_rope, ckv, k_rope


def rg_lru_branch(xb, gb, pos, conv0, h0, conv_w, conv_b, w_rg, b_rg, w_ig, b_ig, lam):
    B, L, W = xb.shape
    xc_in = jnp.concatenate([conv0.astype(xb.dtype), xb], axis=1)
    xc = conv_b + xc_in[:, 0:L] * conv_w[0]
    for j in range(1, CONV_WIDTH):
        xc = xc + xc_in[:, j:j + L] * conv_w[j]
    new_conv = xc_in[:, L:]
    xblk = xc.reshape(B, L, LRU_BLOCKS, LRU_BLOCK_DIM)
    r = jax.nn.sigmoid((jnp.einsum('blhi,hij->blhj', xblk, w_rg).reshape(B, L, W) + b_rg).astype(jnp.float32))
    i = jax.nn.sigmoid((jnp.einsum('blhi,hij->blhj', xblk, w_ig).reshape(B, L, W) + b_ig).astype(jnp.float32))
    log_a = -LRU_C * r * jax.nn.softplus(-lam.astype(jnp.float32))
    a = jnp.exp(log_a)
    mult = jnp.sqrt(-jnp.expm1(2.0 * log_a))
    mult = jnp.where((pos == 0)[None, :, None], 1.0, mult)
    b = mult * i * xc.astype(jnp.float32)
    b = b.at[:, 0].add(a[:, 0] * h0.astype(jnp.float32))
    def combine(e1, e2):
        return e1[0] * e2[0], e2[0] * e1[1] + e2[1]
    _, h = lax.associative_scan(combine, (a, b), axis=1)
    y = h.astype(xb.dtype) * jax.nn.gelu(gb)
    return y, new_conv, h[:, -1]


def merge_branches(o_sb, o_lru, o_mla, gates, b_gate, w_branch, w_out):
    B, L = gates.shape[:2]
    g = jax.nn.sigmoid((gates + b_gate).astype(jnp.float32)).astype(gates.dtype).reshape(B, L, N_BRANCH, D_MODEL)
    br = jnp.stack([o_sb, o_lru, o_mla], axis=2)
    proj = jnp.einsum('blnw,nwd->blnd', br, w_branch)
    return jnp.einsum('bld,de->ble', jnp.sum(g * proj, axis=2), w_out)


def moe_ffn(x, w_router, router_bias, w_g, w_u, w_d):
    B, L = x.shape[:2]
    s = jax.nn.sigmoid(jnp.einsum('bld,de->ble', x, w_router).astype(jnp.float32))
    sel = s + router_bias.astype(jnp.float32)
    grp_score = jnp.sum(lax.top_k(sel.reshape(B, L, N_GROUPS, EXPERTS_PER_GROUP), 2)[0], axis=-1)
    best = jnp.argmax(grp_score, axis=-1)
    emask = jnp.repeat(jax.nn.one_hot(best, N_GROUPS) > 0, EXPERTS_PER_GROUP, axis=-1)
    _, idx = lax.top_k(jnp.where(emask, sel, NEG_INF), TOP_K)
    w = jnp.take_along_axis(s, idx, axis=-1)
    w = w / jnp.sum(w, axis=-1, keepdims=True)
    gate = jnp.einsum('blk,blke->ble', w, jax.nn.one_hot(idx, N_EXPERTS, dtype=w.dtype))
    hg = jnp.einsum('bld,edf->blef', x, w_g)
    hu = jnp.einsum('bld,edf->blef', x, w_u)
    h = jax.nn.silu(hg) * hu * gate[..., None].astype(x.dtype)
    return jnp.einsum('blef,efd->bld', h, w_d)


def mix_prompt(x, pos, lw):
    (w_in, b_gate, conv_w, conv_b, w_rg, b_rg, w_ig, b_ig, lam, q_norm_g, w_uq, kv_norm_g, w_ukv, w_branch, w_out) = lw
    B, T = x.shape[:2]
    qs, ks, vs, lx, lg, dq, dkv, kr, gates = split_in(x, w_in)
    q = qs.reshape(B, T, SB_HEADS, SB_HEAD_DIM)
    k = ks.reshape(B, T, SB_KV_HEADS, SB_HEAD_DIM)
    v = vs.reshape(B, T, SB_KV_HEADS, SB_HEAD_DIM)
    o_sb = blocked_queries(lambda qb, pb: sb_attend(qb, k, v, pb, pos), q, pos)
    conv0 = jnp.zeros((B, CONV_WIDTH - 1, LRU_WIDTH), x.dtype)
    h0 = jnp.zeros((B, LRU_WIDTH), jnp.float32)
    o_lru, conv_new, h_new = rg_lru_branch(lx, lg, pos, conv0, h0, conv_w, conv_b, w_rg, b_rg, w_ig, b_ig, lam)
    q_nope, q_rope, ckv, k_rope = mla_project(dq, dkv, kr, pos, q_norm_g, w_uq, kv_norm_g)
    w_uk, w_uv = split_ukv(w_ukv)
    k_nope = jnp.einsum('btc,chd->bthd', ckv, w_uk)
    v_m = jnp.einsum('btc,chd->bthd', ckv, w_uv)
    qm = jnp.concatenate([q_nope, q_rope], axis=-1)
    km = jnp.concatenate([k_nope, jnp.broadcast_to(k_rope[:, :, None], (B, T, MLA_HEADS, QK_ROPE))], axis=-1)
    o_mla = blocked_queries(lambda qb, pb: softmax_attend(qb, km, v_m, pb, pos), qm, pos)
    mix = merge_branches(o_sb.reshape(B, T, SB_WIDTH), o_lru, o_mla.reshape(B, T, MLA_WIDTH), gates, b_gate, w_branch, w_out)
    return mix, (k, v, ckv, k_rope, h_new, conv_new)


def mix_sample(x, pos, k_past, v_past, c_past, r_past, h0, conv0, lw):
    (w_in, b_gate, conv_w, conv_b, w_rg, b_rg, w_ig, b_ig, lam, q_norm_g, w_uq, kv_norm_g, w_ukv, w_branch, w_out) = lw
    B, L = x.shape[:2]
    qs, ks, vs, lx, lg, dq, dkv, kr, gates = split_in(x, w_in)
    q = qs.reshape(B, L, SB_HEADS, SB_HEAD_DIM)
    k = ks.reshape(B, L, SB_KV_HEADS, SB_HEAD_DIM)
    v = vs.reshape(B, L, SB_KV_HEADS, SB_HEAD_DIM)
    k_all = jnp.concatenate([k_past.astype(k.dtype), k], axis=1)
    v_all = jnp.concatenate([v_past.astype(v.dtype), v], axis=1)
    k_pos = jnp.arange(k_all.shape[1], dtype=jnp.int32)
    o_sb = sb_attend(q, k_all, v_all, pos, k_pos)
    o_lru, conv_new, h_new = rg_lru_branch(lx, lg, pos, conv0, h0, conv_w, conv_b, w_rg, b_rg, w_ig, b_ig, lam)
    q_nope, q_rope, ckv, k_rope = mla_project(dq, dkv, kr, pos, q_norm_g, w_uq, kv_norm_g)
    w_uk, w_uv = split_ukv(w_ukv)
    c_all = jnp.concatenate([c_past.astype(ckv.dtype), ckv], axis=1)
    r_all = jnp.concatenate([r_past.astype(k_rope.dtype), k_rope], axis=1)
    o_mla = mla_latent_attend(q_nope, q_rope, c_all, r_all, w_uk, w_uv, pos, k_pos)
    mix = merge_branches(o_sb.reshape(B, L, SB_WIDTH), o_lru, o_mla.reshape(B, L, MLA_WIDTH), gates, b_gate, w_branch, w_out)
    return mix, (k, v, ckv, k_rope, h_new, conv_new)


def setup_inputs(seed: int = 0) -> dict:
    key = jax.random.key(seed)
    ks = jax.random.split(key, 40)
    f32 = jnp.float32
    n_pages = PAST_LEN // PAGE_SIZE
    n_pool = (DEC_BATCH * n_pages * 5) // 4
    def nrm(k, shape, scale):
        return jax.random.normal(k, shape, f32) * scale
    a0 = jax.random.uniform(ks[13], (DEPTH, LRU_WIDTH), f32, 0.9, 0.999)
    p = a0 ** (1.0 / LRU_C)
    page_table = jax.random.permutation(ks[8], n_pool)[:DEC_BATCH * n_pages].reshape(DEC_BATCH, n_pages).astype(jnp.int32)
    return {
        'x_prompt': jax.random.normal(ks[0], (BATCH, SEQ, D_MODEL), f32),
        'x_sample': jax.random.normal(ks[1], (DEC_BATCH, DEC_SEQ, D_MODEL), f32),
        'cache_sb_k': jax.random.normal(ks[2], (DEPTH, n_pool, PAGE_SIZE, SB_KV_HEADS, SB_HEAD_DIM), f32),
        'cache_sb_v': jax.random.normal(ks[3], (DEPTH, n_pool, PAGE_SIZE, SB_KV_HEADS, SB_HEAD_DIM), f32),
        'cache_mla_ckv': jax.random.normal(ks[4], (DEPTH, n_pool, PAGE_SIZE, KV_LORA), f32),
        'cache_mla_krope': jax.random.normal(ks[5], (DEPTH, n_pool, PAGE_SIZE, QK_ROPE), f32),
        'state_lru_h': nrm(ks[6], (DEPTH, DEC_BATCH, LRU_WIDTH), 0.5),
        'state_lru_conv': jax.random.normal(ks[7], (DEPTH, DEC_BATCH, CONV_WIDTH - 1, LRU_WIDTH), f32),
        'page_table': page_table,
        'meta_tokens': jax.random.normal(ks[9], (N_META, D_MODEL), f32),
        'w_in': nrm(ks[10], (DEPTH, D_MODEL, D_IN), D_MODEL ** -0.5),
        'b_gate': nrm(ks[11], (DEPTH, N_BRANCH * D_MODEL), 0.02),
        'conv_w': nrm(ks[12], (DEPTH, CONV_WIDTH, LRU_WIDTH), CONV_WIDTH ** -0.5),
        'conv_b': nrm(ks[14], (DEPTH, LRU_WIDTH), 0.01),
        'w_rg': nrm(ks[15], (DEPTH, LRU_BLOCKS, LRU_BLOCK_DIM, LRU_BLOCK_DIM), LRU_BLOCK_DIM ** -0.5),
        'b_rg': nrm(ks[16], (DEPTH, LRU_WIDTH), 0.02),
        'w_ig': nrm(ks[17], (DEPTH, LRU_BLOCKS, LRU_BLOCK_DIM, LRU_BLOCK_DIM), LRU_BLOCK_DIM ** -0.5),
        'b_ig': nrm(ks[18], (DEPTH, LRU_WIDTH), 0.02),
        'lru_lambda': jnp.log(p) - jnp.log1p(-p),
        'q_norm_g': 1.0 + nrm(ks[19], (DEPTH, Q_LORA), 0.02),
        'w_uq': nrm(ks[20], (DEPTH, Q_LORA, MLA_HEADS * (QK_NOPE + QK_ROPE)), Q_LORA ** -0.5),
        'kv_norm_g': 1.0 + nrm(ks[21], (DEPTH, KV_LORA), 0.02),
        'w_ukv': nrm(ks[22], (DEPTH, KV_LORA, MLA_HEADS * (QK_NOPE + V_HEAD)), KV_LORA ** -0.5),
        'w_branch': nrm(ks[23], (DEPTH, N_BRANCH, BRANCH_WIDTH, D_MODEL), DEEPNORM_BETA * BRANCH_WIDTH ** -0.5),
        'w_out': nrm(ks[24], (DEPTH, D_MODEL, D_MODEL), DEEPNORM_BETA * D_MODEL ** -0.5),
        'ln1_g': 1.0 + nrm(ks[25], (DEPTH, D_MODEL), 0.02),
        'ln1_b': nrm(ks[26], (DEPTH, D_MODEL), 0.02),
        'ln2_g': 1.0 + nrm(ks[27], (DEPTH, D_MODEL), 0.02),
        'ln2_b': nrm(ks[28], (DEPTH, D_MODEL), 0.02),
        'w_router': nrm(ks[29], (D_MODEL, N_EXPERTS), D_MODEL ** -0.5),
        'router_bias': nrm(ks[30], (N_EXPERTS,), 0.01),
        'w_exp_gate': nrm(ks[31], (DEPTH, N_EXPERTS, D_MODEL, D_EXPERT), DEEPNORM_BETA * D_MODEL ** -0.5),
        'w_exp_up': nrm(ks[32], (DEPTH, N_EXPERTS, D_MODEL, D_EXPERT), DEEPNORM_BETA * D_MODEL ** -0.5),
        'w_exp_down': nrm(ks[33], (DEPTH, N_EXPERTS, D_EXPERT, D_MODEL), DEEPNORM_BETA * D_EXPERT ** -0.5),
    }


def reference(x_prompt, x_sample, cache_sb_k, cache_sb_v, cache_mla_ckv, cache_mla_krope,
              state_lru_h, state_lru_conv, page_table, meta_tokens, w_in, b_gate, conv_w, conv_b,
              w_rg, b_rg, w_ig, b_ig, lru_lambda, q_norm_g, w_uq, kv_norm_g, w_ukv, w_branch, w_out,
              ln1_g, ln1_b, ln2_g, ln2_b, w_router, router_bias, w_exp_gate, w_exp_up, w_exp_down):
    B, S = x_prompt.shape[:2]
    T = N_META + S
    DB, LS = x_sample.shape[:2]
    past_rows = page_table.shape[1] * PAGE_SIZE
    xp = jnp.concatenate([jnp.broadcast_to(meta_tokens.astype(x_prompt.dtype)[None], (B, N_META, D_MODEL)), x_prompt], axis=1)
    xs = x_sample
    pos_p = jnp.arange(T, dtype=jnp.int32)
    pos_s = PAST_LEN + jnp.arange(LS, dtype=jnp.int32)
    st_prompt = []
    st_sample = []
    for l in range(DEPTH):
        lw = (w_in[l], b_gate[l], conv_w[l], conv_b[l], w_rg[l], b_rg[l], w_ig[l], b_ig[l], lru_lambda[l],
              q_norm_g[l], w_uq[l], kv_norm_g[l], w_ukv[l], w_branch[l], w_out[l])
        mix_p, stp = mix_prompt(xp, pos_p, lw)
        xp = layer_norm(DEEPNORM_ALPHA * xp + mix_p, ln1_g[l], ln1_b[l])
        xp = layer_norm(DEEPNORM_ALPHA * xp + moe_ffn(xp, w_router, router_bias, w_exp_gate[l], w_exp_up[l], w_exp_down[l]), ln2_g[l], ln2_b[l])
        st_prompt.append(stp)
        k_past = cache_sb_k[l, page_table].reshape(DB, past_rows, SB_KV_HEADS, SB_HEAD_DIM)
        v_past = cache_sb_v[l, page_table].reshape(DB, past_rows, SB_KV_HEADS, SB_HEAD_DIM)
        c_past = cache_mla_ckv[l, page_table].reshape(DB, past_rows, KV_LORA)
        r_past = cache_mla_krope[l, page_table].reshape(DB, past_rows, QK_ROPE)
        mix_s, sts = mix_sample(xs, pos_s, k_past, v_past, c_past, r_past, state_lru_h[l], state_lru_conv[l], lw)
        xs = layer_norm(DEEPNORM_ALPHA * xs + mix_s, ln1_g[l], ln1_b[l])
        xs = layer_norm(DEEPNORM_ALPHA * xs + moe_ffn(xs, w_router, router_bias, w_exp_gate[l], w_exp_up[l], w_exp_down[l]), ln2_g[l], ln2_b[l])
        st_sample.append(sts)
    p_sb_k, p_sb_v, p_ckv, p_krope, p_lru_h, p_lru_conv = [jnp.stack(f, axis=0) for f in zip(*st_prompt)]
    s_sb_k, s_sb_v, s_ckv, s_krope, s_lru_h, s_lru_conv = [jnp.stack(f, axis=0) for f in zip(*st_sample)]
    y_prompt = xp[:, N_META:]
    y_sample = xs
    return (y_prompt, y_sample, p_sb_k, p_sb_v, p_ckv, p_krope, p_lru_h, p_lru_conv,
            s_sb_k, s_sb_v, s_ckv, s_krope, s_lru_h, s_lru_conv)
```

```python
import functools
import math

import jax
import jax.numpy as jnp
import numpy as np
from jax import lax
from jax.experimental import pallas as pl
from jax.experimental.pallas import tpu as pltpu

F32 = jnp.float32
BF16 = jnp.bfloat16

SB_HEADS = 8
SB_KV_HEADS = 2
SB_GROUP = SB_HEADS // SB_KV_HEADS
SB_HEAD_DIM = 64
LRU_WIDTH = 512
LRU_BLOCKS = 8
CONV_WIDTH = 4
LRU_C = 8.0
MLA_HEADS = 8
Q_LORA = 256
KV_LORA = 128
QK_NOPE = 64
QK_ROPE = 32
V_HEAD = 64
MLA_SCALE = (QK_NOPE + QK_ROPE) ** -0.5
ROPE_THETA = 10000.0
N_BRANCH = 3
BRANCH_WIDTH = 512
N_EXPERTS = 16
N_GROUPS = 4
EXPERTS_PER_GROUP = N_EXPERTS // N_GROUPS
D_EXPERT = 256
LN_EPS = 1e-5
RMS_EPS = 1e-6
NEG_INF = -1e30

LANE = 128
SUBLANE = 8
QB = 128
TM = 256
TM_MOE = 512
ROW_ALIGN = 512
LRU_CHUNK = 128
SB_CHUNK = 256
SB_LOG_FLOOR = -40.0
VMEM_LIMIT = 56 * 1024 * 1024

C_Q = 0
C_K = C_Q + SB_HEADS * LANE
C_V = C_K + LANE
C_LX = C_V + LANE
C_LG = C_LX + LRU_WIDTH
C_DQ = C_LG + LRU_WIDTH
C_DKV = C_DQ + Q_LORA
C_KR = C_DKV + KV_LORA
C_KRR = C_KR + LANE
C_G = C_KRR + LANE


def _const_spec(shape):
    zeros = (0,) * len(shape)
    return pl.BlockSpec(shape, lambda *_: zeros)


def _params(*sem):
    return pltpu.CompilerParams(dimension_semantics=sem, vmem_limit_bytes=VMEM_LIMIT)


def _softplus_neg_abs(z):
    return jnp.log1p(jnp.exp(-jnp.abs(z)))


def _inproj_kernel(x_ref, w_ref, qsb_ref, k_ref, v_ref, kb_ref, vb_ref, lx_ref, lg_ref,
                   dq_ref, dkv_ref, kr_ref, krr_ref, gates_ref):
    xb = x_ref[...].astype(BF16)

    def mm(a, b):
        return jnp.dot(xb, w_ref[:, a:b], preferred_element_type=F32)

    for h in range(SB_HEADS):
        qsb_ref[h] = mm(C_Q + LANE * h, C_Q + LANE * (h + 1)).astype(BF16)
    k = mm(C_K, C_V)
    k_ref[...] = k
    kb_ref[...] = k.astype(BF16)
    v = mm(C_V, C_LX)
    v_ref[...] = v
    vb_ref[...] = v.astype(BF16)
    lx_ref[...] = mm(C_LX, C_LG)
    lg_ref[...] = mm(C_LG, C_DQ)
    dq_ref[...] = mm(C_DQ, C_DKV)
    dkv_ref[...] = mm(C_DKV, C_KR)
    kr_ref[...] = mm(C_KR, C_KRR)
    krr_ref[...] = mm(C_KRR, C_G)
    d = gates_ref.shape[1] // N_BRANCH
    for i in range(N_BRANCH):
        gates_ref[:, i * d:(i + 1) * d] = mm(C_G + i * d, C_G + (i + 1) * d)


def _inproj(x, w):
    n, d = x.shape
    cols = w.shape[1]
    row = lambda width: pl.BlockSpec((TM, width), lambda i: (i, 0))
    sds = lambda width, dt: jax.ShapeDtypeStruct((n, width), dt)
    return pl.pallas_call(
        _inproj_kernel,
        grid=(n // TM,),
        in_specs=[row(d), _const_spec((d, cols))],
        out_specs=[pl.BlockSpec((SB_HEADS, TM, LANE), lambda i: (0, i, 0)),
                   row(LANE), row(LANE), row(LANE), row(LANE), row(LRU_WIDTH), row(LRU_WIDTH),
                   row(Q_LORA), row(KV_LORA), row(LANE), row(LANE), row(N_BRANCH * d)],
        out_shape=[jax.ShapeDtypeStruct((SB_HEADS, n, LANE), BF16),
                   sds(LANE, F32), sds(LANE, F32), sds(LANE, BF16), sds(LANE, BF16),
                   sds(LRU_WIDTH, F32), sds(LRU_WIDTH, F32), sds(Q_LORA, F32), sds(KV_LORA, F32),
                   sds(LANE, F32), sds(LANE, F32), sds(N_BRANCH * d, F32)],
        compiler_params=_params("parallel"),
        name="inproj",
    )(x, w)


def _rms(x, g):
    return x * lax.rsqrt(jnp.mean(jnp.square(x), axis=-1, keepdims=True) + RMS_EPS) * g


def _mlaproj_kernel(dq_ref, dkv_ref, kr_ref, krr_ref, pos_ref, inv_ref, qg_ref, kvg_ref, wuq_ref, wuk_ref,
                    qcat_ref, ckv_ref, krope_ref, kcat_ref):
    nope_w = MLA_HEADS * QK_NOPE
    rope_w = MLA_HEADS * LANE
    qn = _rms(dq_ref[...], qg_ref[...]).astype(BF16)
    q_nope = jnp.dot(qn, wuq_ref[:, :nope_w], preferred_element_type=F32)
    q_r = jnp.dot(qn, wuq_ref[:, nope_w:nope_w + rope_w], preferred_element_type=F32)
    q_rr = jnp.dot(qn, wuq_ref[:, nope_w + rope_w:], preferred_element_type=F32)
    ang = pos_ref[...] * inv_ref[...]
    cos = jnp.cos(ang)
    sin = jnp.sin(ang)
    q_lat = jnp.dot(q_nope.astype(BF16), wuk_ref[...], preferred_element_type=F32)
    for h in range(MLA_HEADS):
        sl = slice(h * LANE, (h + 1) * LANE)
        qcat_ref[h, :, :LANE] = (q_lat[:, sl] * MLA_SCALE).astype(BF16)
        qcat_ref[h, :, LANE:] = ((q_r[:, sl] * cos + q_rr[:, sl] * sin) * MLA_SCALE).astype(BF16)
    ckv = _rms(dkv_ref[...], kvg_ref[...])
    krope = kr_ref[...] * cos + krr_ref[...] * sin
    ckv_ref[...] = ckv
    krope_ref[...] = krope
    kcat_ref[:, :LANE] = ckv.astype(BF16)
    kcat_ref[:, LANE:] = krope.astype(BF16)


def _mlaproj(dq, dkv, kr, krr, pos, inv, qg, kvg, wuq, wuk):
    n = dq.shape[0]
    row = lambda width: pl.BlockSpec((TM, width), lambda i: (i, 0))
    return pl.pallas_call(
        _mlaproj_kernel,
        grid=(n // TM,),
        in_specs=[row(Q_LORA), row(KV_LORA), row(LANE), row(LANE), row(1), _const_spec((1, LANE)),
                  _const_spec((1, Q_LORA)), _const_spec((1, KV_LORA)),
                  _const_spec(wuq.shape), _const_spec(wuk.shape)],
        out_specs=[pl.BlockSpec((MLA_HEADS, TM, 2 * LANE), lambda i: (0, i, 0)),
                   row(KV_LORA), row(LANE), row(2 * LANE)],
        out_shape=[jax.ShapeDtypeStruct((MLA_HEADS, n, 2 * LANE), BF16),
                   jax.ShapeDtypeStruct((n, KV_LORA), F32), jax.ShapeDtypeStruct((n, LANE), F32),
                   jax.ShapeDtypeStruct((n, 2 * LANE), BF16)],
        compiler_params=_params("parallel"),
        name="mlaproj",
    )(dq, dkv, kr, krr, pos, inv, qg, kvg, wuq, wuk)


def _sb_block(z, mask, carry, tri):
    c = z.shape[1]
    t = _softplus_neg_abs(z)
    log_beta = jnp.minimum(z, 0.0) - t
    log_keep = jnp.minimum(-z, 0.0) - t
    if mask is not None:
        log_keep = jnp.where(mask, log_keep, 0.0)
    hi = log_keep.astype(BF16)
    lo = (log_keep - hi.astype(F32)).astype(BF16)
    s = jnp.dot(hi, tri, preferred_element_type=F32) + jnp.dot(lo, tri, preferred_element_type=F32)
    suffix = s[:, :c]
    total = s[:, c:]
    reps = c // LANE
    carry_b = carry if reps == 1 else jnp.concatenate([carry] * reps, axis=1)
    w = jnp.exp(log_beta + suffix + carry_b)
    if mask is not None:
        w = jnp.where(mask, w, 0.0)
    return w, total


def _tri(c):
    r = np.arange(c)
    upper = (r[:, None] > r[None, :]).astype(np.float32)
    return jnp.asarray(np.concatenate([upper, np.ones((c, LANE), np.float32)], axis=1), BF16)


def _sbp_kernel(q_ref, k_ref, v_ref, tri_ref, o_ref, carry_ref, acc_ref):
    i = pl.program_id(1)
    rows = SB_GROUP * QB
    tri = tri_ref[...]
    qpos = lax.broadcasted_iota(jnp.int32, (rows, QB), 0) % QB
    kpos = lax.broadcasted_iota(jnp.int32, (rows, QB), 1)
    diag_mask = kpos < qpos

    for kv in range(SB_KV_HEADS):
        q = q_ref[kv * SB_GROUP:(kv + 1) * SB_GROUP].reshape(rows, LANE)

        def step(j, mask):
            start = pl.multiple_of(j * QB, QB)
            kb = k_ref[pl.ds(start, QB), :]
            vb = v_ref[pl.ds(start, QB), :]
            z = lax.dot_general(q, kb, (((1,), (1,)), ((), ())), preferred_element_type=F32)
            w, total = _sb_block(z, mask, carry_ref[...], tri)
            acc_ref[...] += jnp.dot(w.astype(BF16), vb, preferred_element_type=F32)
            carry_ref[...] += total

        carry_ref[...] = jnp.zeros_like(carry_ref)
        acc_ref[...] = jnp.zeros_like(acc_ref)
        step(i, diag_mask)

        def cond(c):
            j, live = c
            return jnp.logical_and(j >= 0, live > SB_LOG_FLOOR)

        def body(c):
            j, _ = c
            step(j, None)
            return j - 1, jnp.max(carry_ref[...])

        lax.while_loop(cond, body, (i - 1, jnp.max(carry_ref[...])))
        for g in range(SB_GROUP):
            h = kv * SB_GROUP + g
            o_ref[:, h * LANE:(h + 1) * LANE] = acc_ref[g * QB:(g + 1) * QB, :].astype(BF16)


def _sb_prompt(qsb, kb, vb, n_batch, t_pad):
    n = kb.shape[0]
    nq = t_pad // QB
    rows = SB_GROUP * QB
    return pl.pallas_call(
        _sbp_kernel,
        grid=(n_batch, nq),
        in_specs=[pl.BlockSpec((SB_HEADS, QB, LANE), lambda b, i: (0, b * nq + i, 0)),
                  pl.BlockSpec((t_pad, LANE), lambda b, i: (b, 0)),
                  pl.BlockSpec((t_pad, LANE), lambda b, i: (b, 0)),
                  _const_spec((QB, QB + LANE))],
        out_specs=pl.BlockSpec((QB, SB_HEADS * LANE), lambda b, i: (b * nq + i, 0)),
        out_shape=jax.ShapeDtypeStruct((n, SB_HEADS * LANE), BF16),
        scratch_shapes=[pltpu.VMEM((rows, LANE), F32), pltpu.VMEM((rows, LANE), F32)],
        compiler_params=_params("parallel", "arbitrary"),
        name="sb_prompt",
    )(qsb, kb, vb, _tri(QB))


def _mlap_kernel(q_ref, kc_ref, wuv_ref, o_ref, m_ref, l_ref, acc_ref):
    i = pl.program_id(1)
    rows = MLA_HEADS * QB
    q = q_ref[...].reshape(rows, 2 * LANE)
    m_ref[...] = jnp.full_like(m_ref, NEG_INF)
    l_ref[...] = jnp.zeros_like(l_ref)
    acc_ref[...] = jnp.zeros_like(acc_ref)

    def step(j, masked):
        start = pl.multiple_of(j * QB, QB)
        kc = kc_ref[pl.ds(start, QB), :]
        s = lax.dot_general(q, kc, (((1,), (1,)), ((), ())), preferred_element_type=F32)
        if masked:
            qpos = lax.broadcasted_iota(jnp.int32, (rows, QB), 0) % QB
            kpos = lax.broadcasted_iota(jnp.int32, (rows, QB), 1)
            s = jnp.where(kpos <= qpos, s, NEG_INF)
        m_old = m_ref[...]
        m_new = jnp.maximum(m_old, jnp.max(s, axis=1, keepdims=True))
        alpha = jnp.exp(m_old - m_new)
        p = jnp.exp(s - m_new)
        l_ref[...] = alpha * l_ref[...] + jnp.sum(p, axis=1, keepdims=True)
        acc_ref[...] = alpha * acc_ref[...] + jnp.dot(p.astype(BF16), kc[:, :LANE], preferred_element_type=F32)
        m_ref[...] = m_new

    def body(j, c):
        step(j, False)
        return c

    lax.fori_loop(0, i, body, 0)
    step(i, True)
    o_lat = (acc_ref[...] / l_ref[...]).astype(BF16)
    out = jnp.zeros((QB, MLA_HEADS * V_HEAD), F32)
    for h in range(MLA_HEADS):
        out = out + jnp.dot(o_lat[h * QB:(h + 1) * QB], wuv_ref[h], preferred_element_type=F32)
    o_ref[...] = out.astype(BF16)


def _mla_prompt(qcat, kcat, wuv, n_batch, t_pad):
    n = kcat.shape[0]
    nq = t_pad // QB
    rows = MLA_HEADS * QB
    return pl.pallas_call(
        _mlap_kernel,
        grid=(n_batch, nq),
        in_specs=[pl.BlockSpec((MLA_HEADS, QB, 2 * LANE), lambda b, i: (0, b * nq + i, 0)),
                  pl.BlockSpec((t_pad, 2 * LANE), lambda b, i: (b, 0)),
                  _const_spec(wuv.shape)],
        out_specs=pl.BlockSpec((QB, MLA_HEADS * V_HEAD), lambda b, i: (b * nq + i, 0)),
        out_shape=jax.ShapeDtypeStruct((n, MLA_HEADS * V_HEAD), BF16),
        scratch_shapes=[pltpu.VMEM((rows, 1), F32), pltpu.VMEM((rows, 1), F32), pltpu.VMEM((rows, LANE), F32)],
        compiler_params=_params("parallel", "arbitrary"),
        name="mla_prompt",
    )(qcat, kcat, wuv)


def _gelu_tanh(x):
    return 0.5 * x * (1.0 + jnp.tanh(math.sqrt(2.0 / math.pi) * (x + 0.044715 * (x * x * x))))


def _lru_coeffs(xc, wg_ref, brg_ref, big_ref, lam_ref, first_is_start):
    w = xc.shape[1]
    gates = jnp.dot(xc.astype(BF16), wg_ref[...], preferred_element_type=F32)
    r = jax.nn.sigmoid(gates[:, :w] + brg_ref[...])
    i = jax.nn.sigmoid(gates[:, w:] + big_ref[...])
    lam = lam_ref[...]
    softplus_neg_lam = jnp.maximum(-lam, 0.0) + _softplus_neg_abs(lam)
    log_a = -LRU_C * r * softplus_neg_lam
    a = jnp.exp(log_a)
    mult = jnp.sqrt(-jnp.tanh(log_a) * (a * a + 1.0))
    if first_is_start is not None:
        mult = jnp.where(first_is_start, 1.0, mult)
    return a, mult * i * xc


def _lrup_kernel(x_ref, g_ref, cw_ref, cb_ref, wg_ref, brg_ref, big_ref, lam_ref, y_ref, hl_ref,
                 xbuf_ref, h_ref, *, last_row):
    c = pl.program_id(1)
    tc = x_ref.shape[0]
    tail = CONV_WIDTH - 1

    @pl.when(c == 0)
    def _():
        xbuf_ref[:SUBLANE] = jnp.zeros((SUBLANE, xbuf_ref.shape[1]), F32)
        h_ref[...] = jnp.zeros_like(h_ref)

    x = x_ref[...]
    xbuf_ref[SUBLANE:] = x
    xc = cb_ref[...] + xbuf_ref[SUBLANE - tail:SUBLANE - tail + tc] * cw_ref[0:1]
    for j in range(1, CONV_WIDTH):
        xc = xc + xbuf_ref[SUBLANE - tail + j:SUBLANE - tail + j + tc] * cw_ref[j:j + 1]
    xbuf_ref[:SUBLANE] = x[tc - SUBLANE:]

    row = lax.broadcasted_iota(jnp.int32, (tc, 1), 0)
    a, b = _lru_coeffs(xc, wg_ref, brg_ref, big_ref, lam_ref, jnp.logical_and(row == 0, c == 0))
    shift = 1
    while shift < tc:
        a_prev = pltpu.roll(a, shift, 0)
        b_prev = pltpu.roll(b, shift, 0)
        ok = row >= shift
        b = jnp.where(ok, a * b_prev + b, b)
        a = jnp.where(ok, a * a_prev, a)
        shift *= 2
    h = a * h_ref[SUBLANE - 1:SUBLANE] + b
    h_ref[...] = h[tc - SUBLANE:]
    y_ref[...] = (h * _gelu_tanh(g_ref[...])).astype(BF16)

    lc, lr = divmod(last_row, tc)
    base = (lr // SUBLANE) * SUBLANE

    @pl.when(c == lc)
    def _():
        hl_ref[0] = h[base:base + SUBLANE]


def _lru_prompt(lx, lg, cw, cb, wg, brg, big, lam, n_batch, t_pad, t_real):
    n, w = lx.shape
    nc = t_pad // LRU_CHUNK
    row = pl.BlockSpec((LRU_CHUNK, w), lambda b, c: (b * nc + c, 0))
    y, hl = pl.pallas_call(
        functools.partial(_lrup_kernel, last_row=t_real - 1),
        grid=(n_batch, nc),
        in_specs=[row, row, _const_spec(cw.shape), _const_spec(cb.shape), _const_spec(wg.shape),
                  _const_spec(brg.shape), _const_spec(big.shape), _const_spec(lam.shape)],
        out_specs=[row, pl.BlockSpec((1, SUBLANE, w), lambda b, c: (b, 0, 0))],
        out_shape=[jax.ShapeDtypeStruct((n, w), BF16), jax.ShapeDtypeStruct((n_batch, SUBLANE, w), F32)],
        scratch_shapes=[pltpu.VMEM((LRU_CHUNK + SUBLANE, w), F32), pltpu.VMEM((SUBLANE, w), F32)],
        compiler_params=_params("parallel", "arbitrary"),
        name="lru_prompt",
    )(lx, lg, cw, cb, wg, brg, big, lam)
    return y, hl[:, (t_real - 1) % SUBLANE]


def _lrus_kernel(x_ref, g_ref, conv0_ref, h0_ref, cw_ref, cb_ref, wg_ref, brg_ref, big_ref, lam_ref,
                 y_ref, hn_ref, *, first_pos):
    steps = x_ref.shape[0]
    tail = CONV_WIDTH - 1
    xin = [conv0_ref[j] for j in range(tail)] + [x_ref[t] for t in range(steps)]
    h = h0_ref[...]
    for t in range(steps):
        xc = cb_ref[...] + xin[t] * cw_ref[0:1]
        for j in range(1, CONV_WIDTH):
            xc = xc + xin[t + j] * cw_ref[j:j + 1]
        start = None
        if first_pos + t == 0:
            start = jnp.full((xc.shape[0], 1), True)
        a, b = _lru_coeffs(xc, wg_ref, brg_ref, big_ref, lam_ref, start)
        h = a * h + b
        y_ref[t] = (h * _gelu_tanh(g_ref[t])).astype(BF16)
    hn_ref[...] = h


def _lru_sample(lx_t, lg_t, conv0_t, h0, cw, cb, wg, brg, big, lam, first_pos):
    steps, nb, w = lx_t.shape
    args = (lx_t, lg_t, conv0_t, h0, cw, cb, wg, brg, big, lam)
    return pl.pallas_call(
        functools.partial(_lrus_kernel, first_pos=first_pos),
        grid=(1,),
        in_specs=[_const_spec(a.shape) for a in args],
        out_specs=[_const_spec((steps, nb, w)), _const_spec((nb, w))],
        out_shape=[jax.ShapeDtypeStruct((steps, nb, w), BF16), jax.ShapeDtypeStruct((nb, w), F32)],
        compiler_params=_params("arbitrary"),
        name="lru_sample",
    )(*args)


def _layer_norm(x, g, b):
    mu = jnp.mean(x, axis=-1, keepdims=True)
    xc = x - mu
    var = jnp.mean(jnp.square(xc), axis=-1, keepdims=True)
    return xc * lax.rsqrt(var + LN_EPS) * g + b


def _route(sel, s):
    e_idx = lax.broadcasted_iota(jnp.int32, sel.shape, 0)
    rows = [sel[e:e + 1] for e in range(N_EXPERTS)]
    best_score = None
    best = None
    for g in range(N_GROUPS):
        v = rows[g * EXPERTS_PER_GROUP:(g + 1) * EXPERTS_PER_GROUP]
        top2 = None
        for a in range(EXPERTS_PER_GROUP):
            for b in range(a + 1, EXPERTS_PER_GROUP):
                pair = v[a] + v[b]
                top2 = pair if top2 is None else jnp.maximum(top2, pair)
        if g == 0:
            best_score, best = top2, jnp.zeros(top2.shape, jnp.int32)
        else:
            better = top2 > best_score
            best = jnp.where(better, g, best)
            best_score = jnp.where(better, top2, best_score)
    masked = jnp.where(e_idx // EXPERTS_PER_GROUP == best, sel, NEG_INF)
    m1 = jnp.max(masked, axis=0, keepdims=True)
    i1 = jnp.min(jnp.where(masked == m1, e_idx, N_EXPERTS), axis=0, keepdims=True)
    rest = jnp.where(e_idx == i1, -jnp.inf, masked)
    m2 = jnp.max(rest, axis=0, keepdims=True)
    i2 = jnp.min(jnp.where(rest == m2, e_idx, N_EXPERTS), axis=0, keepdims=True)
    pick1 = e_idx == i1
    pick2 = e_idx == i2
    w1 = jnp.sum(jnp.where(pick1, s, 0.0), axis=0, keepdims=True)
    w2 = jnp.sum(jnp.where(pick2, s, 0.0), axis=0, keepdims=True)
    denom = w1 + w2
    return jnp.where(pick1, w1 / denom, 0.0) + jnp.where(pick2, w2 / denom, 0.0)


def _merge_kernel(x_ref, osb_ref, olru_ref, omla_ref, gates_ref, bg_ref, wb0_ref, wb1_ref, wb2_ref, wo_ref,
                  g1_ref, b1_ref, wr_ref, rb_ref, x1_ref, gate_ref, *, alpha):
    d = x_ref.shape[1]
    mixed = None
    for i, (o_ref, w_ref) in enumerate(((osb_ref, wb0_ref), (olru_ref, wb1_ref), (omla_ref, wb2_ref))):
        sl = slice(i * d, (i + 1) * d)
        g = jax.nn.sigmoid(gates_ref[:, sl] + bg_ref[:, sl])
        term = g * jnp.dot(o_ref[...], w_ref[...], preferred_element_type=F32)
        mixed = term if mixed is None else mixed + term
    mix = jnp.dot(mixed.astype(BF16), wo_ref[...], preferred_element_type=F32)
    x1 = _layer_norm(alpha * x_ref[...] + mix, g1_ref[...], b1_ref[...])
    x1_ref[...] = x1
    logits = lax.dot_general(wr_ref[...], x1, (((1,), (1,)), ((), ())),
                             precision=lax.Precision.HIGHEST, preferred_element_type=F32)
    s = jax.nn.sigmoid(logits)
    gate_ref[...] = _route(s + rb_ref[...], s)


def _merge(x, osb, olru, omla, gates, bg, wb0, wb1, wb2, wo, g1, b1, wr, rb, alpha):
    n, d = x.shape
    row = lambda width: pl.BlockSpec((TM, width), lambda i: (i, 0))
    consts = (bg, wb0, wb1, wb2, wo, g1, b1, wr, rb)
    return pl.pallas_call(
        functools.partial(_merge_kernel, alpha=alpha),
        grid=(n // TM,),
        in_specs=[row(d), row(osb.shape[1]), row(olru.shape[1]), row(omla.shape[1]), row(gates.shape[1])]
        + [_const_spec(c.shape) for c in consts],
        out_specs=[row(d), pl.BlockSpec((N_EXPERTS, TM), lambda i: (0, i))],
        out_shape=[jax.ShapeDtypeStruct((n, d), F32), jax.ShapeDtypeStruct((N_EXPERTS, n), F32)],
        compiler_params=_params("parallel"),
        name="merge",
    )(x, osb, olru, omla, gates, *consts)


def _moe_kernel(x_ref, gate_ref, wg_ref, wu_ref, wd_ref, g2_ref, b2_ref, o_ref, xb_ref, acc_ref, *, alpha):
    e = pl.program_id(1)

    @pl.when(e == 0)
    def _():
        xb_ref[...] = x_ref[...].astype(BF16)
        acc_ref[...] = jnp.zeros_like(acc_ref)

    xb = xb_ref[...]
    hg = jnp.dot(xb, wg_ref[0, 0], preferred_element_type=F32)
    hu = jnp.dot(xb, wu_ref[0, 0], preferred_element_type=F32)
    gate = gate_ref[...]
    lane = lax.broadcasted_iota(jnp.int32, gate.shape, 1)
    gcol = jnp.sum(jnp.where(lane == e, gate, 0.0), axis=1, keepdims=True)
    h = (hg * jax.nn.sigmoid(hg)) * hu * gcol
    acc_ref[...] += jnp.dot(h.astype(BF16), wd_ref[0, 0], preferred_element_type=F32)

    @pl.when(e == pl.num_programs(1) - 1)
    def _():
        o_ref[...] = _layer_norm(alpha * x_ref[...] + acc_ref[...], g2_ref[...], b2_ref[...])


def _moe(x, gate, wg, wu, wd, layer, g2, b2, alpha):
    n, d = x.shape
    f = wg.shape[-1]
    return pl.pallas_call(
        functools.partial(_moe_kernel, alpha=alpha),
        grid=(n // TM_MOE, N_EXPERTS),
        in_specs=[pl.BlockSpec((TM_MOE, d), lambda i, e: (i, 0)),
                  pl.BlockSpec((TM_MOE, N_EXPERTS), lambda i, e: (i, 0)),
                  pl.BlockSpec((1, 1, d, f), lambda i, e: (layer, e, 0, 0)),
                  pl.BlockSpec((1, 1, d, f), lambda i, e: (layer, e, 0, 0)),
                  pl.BlockSpec((1, 1, f, d), lambda i, e: (layer, e, 0, 0)),
                  _const_spec(g2.shape), _const_spec(b2.shape)],
        out_specs=pl.BlockSpec((TM_MOE, d), lambda i, e: (i, 0)),
        out_shape=jax.ShapeDtypeStruct((n, d), F32),
        scratch_shapes=[pltpu.VMEM((TM_MOE, d), BF16), pltpu.VMEM((TM_MOE, d), F32)],
        compiler_params=_params("parallel", "arbitrary"),
        name="moe",
    )(x, gate, wg, wu, wd, g2, b2)


def _page_copies(pt_ref, seq, slot, layer, n_pages, page, kc_ref, vc_ref, cc_ref, rc_ref,
                 kt_ref, vt_ref, cb_ref, rt_ref, sem_ref):
    copies = []
    for j in range(n_pages):
        pid = pt_ref[seq, j]
        lanes = pl.ds(j * page, page)
        copies.append(pltpu.make_async_copy(kc_ref.at[layer, pid], kt_ref.at[slot, :, :, lanes], sem_ref.at[slot, 0]))
        copies.append(pltpu.make_async_copy(vc_ref.at[layer, pid], vt_ref.at[slot, :, :, lanes], sem_ref.at[slot, 1]))
        copies.append(pltpu.make_async_copy(cc_ref.at[layer, pid], cb_ref.at[slot, pl.ds(j * page, page), :], sem_ref.at[slot, 2]))
        copies.append(pltpu.make_async_copy(rc_ref.at[layer, pid], rt_ref.at[slot, :, lanes], sem_ref.at[slot, 3]))
    return copies


def _samp_kernel(pt_ref, qsb_ref, knew_ref, vnew_ref, qcat_ref, kcnew_ref, tri_ref, trin_ref, wuv_ref,
                 kc_ref, vc_ref, cc_ref, rc_ref, osb_ref, omla_ref,
                 kt_ref, vt_ref, cb_ref, rt_ref, sem_ref, carry_ref, acc_ref, *, layer, n_pages, page, steps):
    s = pl.program_id(0)
    ns = pl.num_programs(0)
    slot = s % 2
    past = n_pages * page
    bufs = (kt_ref, vt_ref, cb_ref, rt_ref, sem_ref)
    caches = (kc_ref, vc_ref, cc_ref, rc_ref)

    @pl.when(s == 0)
    def _():
        for cp in _page_copies(pt_ref, 0, 0, layer, n_pages, page, *caches, *bufs):
            cp.start()

    @pl.when(s + 1 < ns)
    def _():
        for cp in _page_copies(pt_ref, s + 1, 1 - slot, layer, n_pages, page, *caches, *bufs):
            cp.start()

    for cp in _page_copies(pt_ref, s, slot, layer, n_pages, page, *caches, *bufs):
        cp.wait()

    rows = SB_GROUP * SUBLANE
    tq = lax.broadcasted_iota(jnp.int32, (rows, LANE), 0) % SUBLANE
    tk = lax.broadcasted_iota(jnp.int32, (rows, LANE), 1)
    new_mask = tk < tq
    tri = tri_ref[...]
    n_chunks = past // SB_CHUNK
    for kv in range(SB_KV_HEADS):
        q = qsb_ref[0, kv * SB_GROUP:(kv + 1) * SB_GROUP].reshape(rows, LANE).astype(BF16)
        z = lax.dot_general(q, knew_ref[0], (((1,), (1,)), ((), ())), preferred_element_type=F32)
        w, total = _sb_block(z, new_mask, jnp.zeros((rows, LANE), F32), trin_ref[...])
        acc_ref[...] = jnp.dot(w.astype(BF16), vnew_ref[0], preferred_element_type=F32)
        carry_ref[...] = total
        qk = q[:, kv * SB_HEAD_DIM:(kv + 1) * SB_HEAD_DIM]

        def cond(c):
            j, live = c
            return jnp.logical_and(j >= 0, live > SB_LOG_FLOOR)

        def body(c):
            j, _ = c
            start = pl.multiple_of(j * SB_CHUNK, SB_CHUNK)
            kt = kt_ref[slot, kv, :, pl.ds(start, SB_CHUNK)].astype(BF16)
            vt = vt_ref[slot, kv, :, pl.ds(start, SB_CHUNK)].astype(BF16)
            z = jnp.dot(qk, kt, preferred_element_type=F32)
            w, total = _sb_block(z, None, carry_ref[...], tri)
            pv = lax.dot_general(w.astype(BF16), vt, (((1,), (1,)), ((), ())), preferred_element_type=F32)
            lanes = slice(kv * SB_HEAD_DIM, (kv + 1) * SB_HEAD_DIM)
            acc_ref[:, lanes] += pv
            carry_ref[...] += total
            return j - 1, jnp.max(carry_ref[...])

        lax.while_loop(cond, body, (n_chunks - 1, jnp.max(carry_ref[...])))
        osb_ref[0, kv] = acc_ref[...]

    mrows = MLA_HEADS * SUBLANE
    q = qcat_ref[0].reshape(mrows, 2 * LANE).astype(BF16)
    cb = cb_ref[slot].astype(BF16)
    rt = rt_ref[slot].astype(BF16)
    s_past = lax.dot_general(q[:, :LANE], cb, (((1,), (1,)), ((), ())), preferred_element_type=F32)
    s_past = s_past + jnp.dot(q[:, LANE:LANE + QK_ROPE], rt, preferred_element_type=F32)
    kcn = kcnew_ref[0]
    s_new = lax.dot_general(q, kcn, (((1,), (1,)), ((), ())), preferred_element_type=F32)
    tq = lax.broadcasted_iota(jnp.int32, (mrows, LANE), 0) % SUBLANE
    tk = lax.broadcasted_iota(jnp.int32, (mrows, LANE), 1)
    s_new = jnp.where(jnp.logical_and(tk <= tq, tk < steps), s_new, NEG_INF)
    m = jnp.maximum(jnp.max(s_past, axis=1, keepdims=True), jnp.max(s_new, axis=1, keepdims=True))
    p_past = jnp.exp(s_past - m)
    p_new = jnp.exp(s_new - m)
    denom = jnp.sum(p_past, axis=1, keepdims=True) + jnp.sum(p_new, axis=1, keepdims=True)
    o_lat = jnp.dot(p_past.astype(BF16), cb, preferred_element_type=F32)
    o_lat = o_lat + jnp.dot(p_new.astype(BF16), kcn[:, :LANE], preferred_element_type=F32)
    o_lat = o_lat / denom
    out = jnp.zeros((SUBLANE, MLA_HEADS * V_HEAD), F32)
    for h in range(MLA_HEADS):
        out = out + jnp.dot(o_lat[h * SUBLANE:(h + 1) * SUBLANE].astype(BF16), wuv_ref[h],
                            preferred_element_type=F32)
    omla_ref[0] = out


def _sample_attn(page_table, qsb_s, knew, vnew, qcat_s, kcnew, wuv, kc, vc, cc, rc, layer, steps):
    nseq, n_pages = page_table.shape
    page = cc.shape[2]
    past = n_pages * page
    assert past % SB_CHUNK == 0
    rows = SB_GROUP * SUBLANE
    any_spec = pl.BlockSpec(memory_space=pl.ANY)
    seq4 = lambda a, b, c: pl.BlockSpec((1, a, b, c), lambda s, pt: (s, 0, 0, 0))
    seq3 = lambda a, b: pl.BlockSpec((1, a, b), lambda s, pt: (s, 0, 0))
    const = lambda shape: pl.BlockSpec(shape, lambda s, pt: (0,) * len(shape))
    grid_spec = pltpu.PrefetchScalarGridSpec(
        num_scalar_prefetch=1,
        grid=(nseq,),
        in_specs=[seq4(SB_HEADS, SUBLANE, LANE), seq3(LANE, LANE), seq3(LANE, LANE),
                  seq4(MLA_HEADS, SUBLANE, 2 * LANE), seq3(LANE, 2 * LANE),
                  const((SB_CHUNK, SB_CHUNK + LANE)), const((LANE, 2 * LANE)), const(wuv.shape),
                  any_spec, any_spec, any_spec, any_spec],
        out_specs=[seq4(SB_KV_HEADS, rows, LANE), seq3(SUBLANE, MLA_HEADS * V_HEAD)],
        scratch_shapes=[pltpu.VMEM((2, SB_KV_HEADS, SB_HEAD_DIM, past), F32),
                        pltpu.VMEM((2, SB_KV_HEADS, SB_HEAD_DIM, past), F32),
                        pltpu.VMEM((2, past, KV_LORA), F32),
                        pltpu.VMEM((2, QK_ROPE, past), F32),
                        pltpu.SemaphoreType.DMA((2, 4)),
                        pltpu.VMEM((rows, LANE), F32), pltpu.VMEM((rows, LANE), F32)])
    return pl.pallas_call(
        functools.partial(_samp_kernel, layer=layer, n_pages=n_pages, page=page, steps=steps),
        grid_spec=grid_spec,
        out_shape=[jax.ShapeDtypeStruct((nseq, SB_KV_HEADS, rows, LANE), F32),
                   jax.ShapeDtypeStruct((nseq, SUBLANE, MLA_HEADS * V_HEAD), F32)],
        compiler_params=_params("arbitrary"),
        name="sample_attn",
    )(page_table, qsb_s, knew, vnew, qcat_s, kcnew, _tri(SB_CHUNK), _tri(LANE), wuv, kc, vc, cc, rc)


def _pad_last(a, width):
    return jnp.pad(a, [(0, 0)] * (a.ndim - 1) + [(0, width - a.shape[-1])])


def _rot_half(w):
    half = w.shape[-1] // 2
    return jnp.concatenate([-w[..., half:], w[..., :half]], axis=-1)


def _pack_w_in(w_in):
    depth, d, _ = w_in.shape
    widths = (SB_HEADS * SB_HEAD_DIM, SB_KV_HEADS * SB_HEAD_DIM, SB_KV_HEADS * SB_HEAD_DIM, LRU_WIDTH, LRU_WIDTH,
              Q_LORA, KV_LORA, QK_ROPE, N_BRANCH * d)
    offs = np.concatenate([[0], np.cumsum(widths)])
    wq, wk, wv, wlx, wlg, wdq, wdkv, wkr, wgt = [w_in[..., offs[i]:offs[i + 1]] for i in range(len(widths))]
    wq = wq.reshape(depth, d, SB_HEADS, SB_HEAD_DIM) * (SB_HEAD_DIM ** -0.5)
    zero = jnp.zeros_like(wq)
    first_kv = (jnp.arange(SB_HEADS) // SB_GROUP == 0)[None, None, :, None]
    wq = jnp.where(first_kv, jnp.concatenate([wq, zero], -1), jnp.concatenate([zero, wq], -1))
    wq = wq.reshape(depth, d, SB_HEADS * LANE)
    packed = jnp.concatenate([wq, wk, wv, wlx, wlg, wdq, wdkv, _pad_last(wkr, LANE), _pad_last(_rot_half(wkr), LANE), wgt],
                             axis=-1)
    return packed.astype(BF16)


def _pack_w_uq(w_uq):
    depth = w_uq.shape[0]
    w = w_uq.reshape(depth, Q_LORA, MLA_HEADS, QK_NOPE + QK_ROPE)
    nope = w[..., :QK_NOPE].reshape(depth, Q_LORA, MLA_HEADS * QK_NOPE)
    rope = w[..., QK_NOPE:]
    rope_p = _pad_last(rope, LANE).reshape(depth, Q_LORA, MLA_HEADS * LANE)
    rot_p = _pad_last(_rot_half(rope), LANE).reshape(depth, Q_LORA, MLA_HEADS * LANE)
    return jnp.concatenate([nope, rope_p, rot_p], axis=-1).astype(BF16)


def _pack_w_ukv(w_ukv):
    depth = w_ukv.shape[0]
    w = w_ukv.reshape(depth, KV_LORA, MLA_HEADS, QK_NOPE + V_HEAD)
    eye = jnp.eye(MLA_HEADS, dtype=w.dtype)
    w_uk_t = jnp.transpose(w[..., :QK_NOPE], (0, 2, 3, 1))
    wuk_bd = w_uk_t[:, :, :, None, :] * eye[None, :, None, :, None]
    wuk_bd = wuk_bd.reshape(depth, MLA_HEADS * QK_NOPE, MLA_HEADS * KV_LORA)
    w_uv = jnp.transpose(w[..., QK_NOPE:], (0, 2, 1, 3))
    wuv_p = w_uv[:, :, :, None, :] * eye[None, :, None, :, None]
    wuv_p = wuv_p.reshape(depth, MLA_HEADS, KV_LORA, MLA_HEADS * V_HEAD)
    return wuk_bd.astype(BF16), wuv_p.astype(BF16)


def _pack_lru_gates(w_rg, w_ig):
    depth = w_rg.shape[0]
    bd = w_rg.shape[-1]
    eye = jnp.eye(LRU_BLOCKS, dtype=w_rg.dtype)

    def block_diag(w):
        full = w[:, :, :, None, :] * eye[None, :, None, :, None]
        return full.reshape(depth, LRU_BLOCKS * bd, LRU_BLOCKS * bd)

    return jnp.concatenate([block_diag(w_rg), block_diag(w_ig)], axis=-1).astype(BF16)


def _pack_w_branch0(w):
    depth, _, d = w.shape
    w = w.reshape(depth, SB_HEADS, SB_HEAD_DIM, d)
    zero = jnp.zeros_like(w)
    first_kv = (jnp.arange(SB_HEADS) // SB_GROUP == 0)[None, :, None, None]
    w = jnp.where(first_kv, jnp.concatenate([w, zero], 2), jnp.concatenate([zero, w], 2))
    return w.reshape(depth, SB_HEADS * LANE, d).astype(BF16)


@jax.jit
def kernel(x_prompt, x_sample, cache_sb_k, cache_sb_v, cache_mla_ckv, cache_mla_krope, state_lru_h, state_lru_conv,
           page_table, meta_tokens, w_in, b_gate, conv_w, conv_b, w_rg, b_rg, w_ig, b_ig, lru_lambda, q_norm_g, w_uq,
           kv_norm_g, w_ukv, w_branch, w_out, ln1_g, ln1_b, ln2_g, ln2_b, w_router, router_bias, w_exp_gate,
           w_exp_up, w_exp_down):
    nb, seq, d = x_prompt.shape
    n_meta = meta_tokens.shape[0]
    t_real = n_meta + seq
    t_pad = -(-t_real // QB) * QB
    db, steps, _ = x_sample.shape
    depth = w_in.shape[0]
    n_pages = page_table.shape[1]
    page = cache_mla_ckv.shape[2]
    past = n_pages * page
    n_prompt = nb * t_pad
    n_samp = db * steps
    n_tok = -(-(n_prompt + n_samp) // ROW_ALIGN) * ROW_ALIGN
    alpha = (2 * depth) ** 0.25
    assert steps <= SUBLANE and t_pad % LRU_CHUNK == 0

    xp = jnp.concatenate([jnp.broadcast_to(meta_tokens[None], (nb, n_meta, d)), x_prompt,
                          jnp.zeros((nb, t_pad - t_real, d), F32)], axis=1).reshape(n_prompt, d)
    x = jnp.concatenate([xp, x_sample.reshape(n_samp, d), jnp.zeros((n_tok - n_prompt - n_samp, d), F32)], axis=0)
    pos_p = jnp.tile(jnp.arange(t_pad, dtype=F32), nb)
    pos_s = jnp.tile(past + jnp.arange(steps, dtype=F32), db)
    pos = jnp.concatenate([pos_p, pos_s, jnp.zeros((n_tok - n_prompt - n_samp,), F32)])[:, None]
    half = QK_ROPE // 2
    inv = ROPE_THETA ** (-jnp.arange(half, dtype=F32) / half)
    inv = jnp.tile(inv, LANE // half)[None]

    w_in_p = _pack_w_in(w_in)
    wuq_p = _pack_w_uq(w_uq)
    wuk_bd, wuv_p = _pack_w_ukv(w_ukv)
    wgates = _pack_lru_gates(w_rg, w_ig)
    wb0 = _pack_w_branch0(w_branch[:, 0])
    wb1 = w_branch[:, 1].astype(BF16)
    wb2 = w_branch[:, 2].astype(BF16)
    wo = w_out.astype(BF16)
    weg = w_exp_gate.astype(BF16)
    weu = w_exp_up.astype(BF16)
    wed = w_exp_down.astype(BF16)
    wr_t = jnp.transpose(w_router)
    rb = router_bias[:, None]
    kc = jnp.transpose(cache_sb_k, (0, 1, 3, 4, 2))
    vc = jnp.transpose(cache_sb_v, (0, 1, 3, 4, 2))
    rc = jnp.transpose(cache_mla_krope, (0, 1, 3, 2))
    conv0_t = jnp.transpose(state_lru_conv, (0, 2, 1, 3))

    def samp_rows(a):
        return a[..., n_prompt:n_prompt + n_samp, :]

    def per_seq_heads(a):
        h, _, c = a.shape
        a = a.reshape(h, db, steps, c)
        a = jnp.pad(a, ((0, 0), (0, 0), (0, SUBLANE - steps), (0, 0)))
        return jnp.transpose(a, (1, 0, 2, 3)).astype(F32)

    def per_seq_keys(a):
        a = a.reshape(db, steps, a.shape[-1])
        return jnp.pad(a, ((0, 0), (0, LANE - steps), (0, 0)))

    st_p = []
    st_s = []
    for l in range(depth):
        (qsb, k, v, kb, vb, lx, lg, dq, dkv, kr, krr, gates) = _inproj(x, w_in_p[l])
        qcat, ckv, krope, kcat = _mlaproj(dq, dkv, kr, krr, pos, inv, q_norm_g[l][None], kv_norm_g[l][None],
                                          wuq_p[l], wuk_bd[l])
        lru_w = (conv_w[l], conv_b[l][None], wgates[l], b_rg[l][None], b_ig[l][None], lru_lambda[l][None])

        osb = _sb_prompt(qsb, kb, vb, nb, t_pad)
        omla = _mla_prompt(qcat, kcat, wuv_p[l], nb, t_pad)
        olru, h_p = _lru_prompt(lx, lg, *lru_w, nb, t_pad, t_real)

        osb_s, omla_s = _sample_attn(page_table, per_seq_heads(samp_rows(qsb)), per_seq_keys(samp_rows(kb)),
                                     per_seq_keys(samp_rows(vb)), per_seq_heads(samp_rows(qcat)),
                                     per_seq_keys(samp_rows(kcat)), wuv_p[l], kc, vc, cache_mla_ckv, rc, l, steps)
        lx_s = samp_rows(lx).reshape(db, steps, LRU_WIDTH)
        lx_t = jnp.transpose(lx_s, (1, 0, 2))
        lg_t = jnp.transpose(samp_rows(lg).reshape(db, steps, LRU_WIDTH), (1, 0, 2))
        olru_t, h_s = _lru_sample(lx_t, lg_t, conv0_t[l], state_lru_h[l], *lru_w, past)

        osb_s = osb_s.reshape(db, SB_KV_HEADS, SB_GROUP, SUBLANE, LANE)[:, :, :, :steps]
        osb_s = jnp.transpose(osb_s, (0, 3, 1, 2, 4)).reshape(n_samp, SB_HEADS * LANE).astype(BF16)
        omla_s = omla_s[:, :steps].reshape(n_samp, MLA_HEADS * V_HEAD).astype(BF16)
        olru_s = jnp.transpose(olru_t, (1, 0, 2)).reshape(n_samp, LRU_WIDTH)
        osb = lax.dynamic_update_slice(osb, osb_s, (n_prompt, 0))
        omla = lax.dynamic_update_slice(omla, omla_s, (n_prompt, 0))
        olru = lax.dynamic_update_slice(olru, olru_s, (n_prompt, 0))

        x1, gate_t = _merge(x, osb, olru, omla, gates, b_gate[l][None], wb0[l], wb1[l], wb2[l], wo[l],
                            ln1_g[l][None], ln1_b[l][None], wr_t, rb, alpha)
        x = _moe(x1, jnp.transpose(gate_t), weg, weu, wed, l, ln2_g[l][None], ln2_b[l][None], alpha)

        def prompt_state(a, width):
            return a[:n_prompt].reshape(nb, t_pad, -1)[:, :t_real, :width]

        conv_in_s = jnp.concatenate([state_lru_conv[l], lx_s], axis=1)
        st_p.append((prompt_state(k, LANE).reshape(nb, t_real, SB_KV_HEADS, SB_HEAD_DIM),
                     prompt_state(v, LANE).reshape(nb, t_real, SB_KV_HEADS, SB_HEAD_DIM),
                     prompt_state(ckv, KV_LORA), prompt_state(krope, QK_ROPE), h_p,
                     prompt_state(lx, LRU_WIDTH)[:, t_real - (CONV_WIDTH - 1):]))
        st_s.append((samp_rows(k).reshape(db, steps, SB_KV_HEADS, SB_HEAD_DIM),
                     samp_rows(v).reshape(db, steps, SB_KV_HEADS, SB_HEAD_DIM),
                     samp_rows(ckv).reshape(db, steps, KV_LORA),
                     samp_rows(krope)[:, :QK_ROPE].reshape(db, steps, QK_ROPE), h_s,
                     conv_in_s[:, steps:]))

    p_state = [jnp.stack(f, axis=0) for f in zip(*st_p)]
    s_state = [jnp.stack(f, axis=0) for f in zip(*st_s)]
    y_prompt = x[:n_prompt].reshape(nb, t_pad, d)[:, n_meta:t_real]
    y_sample = x[n_prompt:n_prompt + n_samp].reshape(db, steps, d)
    return (y_prompt, y_sample, *p_state, *s_state)
```

```python
import functools
import math

import jax
import jax.numpy as jnp
import numpy as np
from jax import lax
from jax.experimental import pallas as pl
from jax.experimental.pallas import tpu as pltpu

F32 = jnp.float32
BF16 = jnp.bfloat16

SB_HEADS = 8
SB_KV_HEADS = 2
SB_GROUP = SB_HEADS // SB_KV_HEADS
SB_HEAD_DIM = 64
LRU_WIDTH = 512
LRU_BLOCKS = 8
CONV_WIDTH = 4
LRU_C = 8.0
MLA_HEADS = 8
Q_LORA = 256
KV_LORA = 128
QK_NOPE = 64
QK_ROPE = 32
V_HEAD = 64
MLA_SCALE = (QK_NOPE + QK_ROPE) ** -0.5
MLA_Q_SCALE = MLA_SCALE * math.log2(math.e)
ROPE_THETA = 10000.0
N_BRANCH = 3
BRANCH_WIDTH = 512
N_EXPERTS = 16
N_GROUPS = 4
EXPERTS_PER_GROUP = N_EXPERTS // N_GROUPS
D_EXPERT = 256
LN_EPS = 1e-5
RMS_EPS = 1e-6
NEG_INF = -1e30

LANE = 128
SUBLANE = 8
QB = 128
MLA_KB = 256
TM = 256
TM_MOE = 512
ROW_ALIGN = 512
LRU_CHUNK = 128
SB_CHUNK = 256
SB_LOG_FLOOR = -40.0
VMEM_LIMIT = 56 * 1024 * 1024

C_Q = 0
C_K = C_Q + SB_HEADS * LANE
C_V = C_K + LANE
C_LX = C_V + LANE
C_LG = C_LX + LRU_WIDTH
C_DQ = C_LG + LRU_WIDTH
C_DKV = C_DQ + Q_LORA
C_KR = C_DKV + KV_LORA
C_KRR = C_KR + LANE
C_G = C_KRR + LANE


def _const_spec(shape):
    zeros = (0,) * len(shape)
    return pl.BlockSpec(shape, lambda *_: zeros)


def _params(*sem):
    return pltpu.CompilerParams(dimension_semantics=sem, vmem_limit_bytes=VMEM_LIMIT)


def _softplus_neg_abs(z):
    return jnp.log1p(jnp.exp(-jnp.abs(z)))


def _inproj_kernel(x_ref, w_ref, qsb_ref, k_ref, v_ref, kb_ref, vb_ref, lx_ref, lg_ref,
                   dq_ref, dkv_ref, kr_ref, krr_ref, gates_ref):
    xb = x_ref[...].astype(BF16)

    def mm(a, b):
        return jnp.dot(xb, w_ref[:, a:b], preferred_element_type=F32)

    for h in range(SB_HEADS):
        qsb_ref[h] = mm(C_Q + LANE * h, C_Q + LANE * (h + 1)).astype(BF16)
    k = mm(C_K, C_V)
    k_ref[...] = k
    kb_ref[...] = k.astype(BF16)
    v = mm(C_V, C_LX)
    v_ref[...] = v
    vb_ref[...] = v.astype(BF16)
    lx_ref[...] = mm(C_LX, C_LG)
    lg_ref[...] = mm(C_LG, C_DQ)
    dq_ref[...] = mm(C_DQ, C_DKV)
    dkv_ref[...] = mm(C_DKV, C_KR)
    kr_ref[...] = mm(C_KR, C_KRR)
    krr_ref[...] = mm(C_KRR, C_G)
    d = gates_ref.shape[1] // N_BRANCH
    for i in range(N_BRANCH):
        gates_ref[:, i * d:(i + 1) * d] = mm(C_G + i * d, C_G + (i + 1) * d)


def _inproj(x, w):
    n, d = x.shape
    cols = w.shape[1]
    row = lambda width: pl.BlockSpec((TM, width), lambda i: (i, 0))
    sds = lambda width, dt: jax.ShapeDtypeStruct((n, width), dt)
    return pl.pallas_call(
        _inproj_kernel,
        grid=(n // TM,),
        in_specs=[row(d), _const_spec((d, cols))],
        out_specs=[pl.BlockSpec((SB_HEADS, TM, LANE), lambda i: (0, i, 0)),
                   row(LANE), row(LANE), row(LANE), row(LANE), row(LRU_WIDTH), row(LRU_WIDTH),
                   row(Q_LORA), row(KV_LORA), row(LANE), row(LANE), row(N_BRANCH * d)],
        out_shape=[jax.ShapeDtypeStruct((SB_HEADS, n, LANE), BF16),
                   sds(LANE, F32), sds(LANE, F32), sds(LANE, BF16), sds(LANE, BF16),
                   sds(LRU_WIDTH, F32), sds(LRU_WIDTH, F32), sds(Q_LORA, F32), sds(KV_LORA, F32),
                   sds(LANE, F32), sds(LANE, F32), sds(N_BRANCH * d, F32)],
        compiler_params=_params("parallel"),
        name="inproj",
    )(x, w)


def _rms(x, g):
    return x * lax.rsqrt(jnp.mean(jnp.square(x), axis=-1, keepdims=True) + RMS_EPS) * g


def _mlaproj_kernel(dq_ref, dkv_ref, kr_ref, krr_ref, pos_ref, inv_ref, qg_ref, kvg_ref, wuq_ref, wuk_ref,
                    qcat_ref, ckv_ref, krope_ref, kcat_ref):
    nope_w = MLA_HEADS * QK_NOPE
    rope_w = MLA_HEADS * LANE
    qn = _rms(dq_ref[...], qg_ref[...]).astype(BF16)
    q_nope = jnp.dot(qn, wuq_ref[:, :nope_w], preferred_element_type=F32)
    q_r = jnp.dot(qn, wuq_ref[:, nope_w:nope_w + rope_w], preferred_element_type=F32)
    q_rr = jnp.dot(qn, wuq_ref[:, nope_w + rope_w:], preferred_element_type=F32)
    ang = pos_ref[...] * inv_ref[...]
    cos = jnp.cos(ang)
    sin = jnp.sin(ang)
    q_lat = jnp.dot(q_nope.astype(BF16), wuk_ref[...], preferred_element_type=F32)
    for h in range(MLA_HEADS):
        sl = slice(h * LANE, (h + 1) * LANE)
        qcat_ref[h, :, :LANE] = (q_lat[:, sl] * MLA_Q_SCALE).astype(BF16)
        qcat_ref[h, :, LANE:] = ((q_r[:, sl] * cos + q_rr[:, sl] * sin) * MLA_Q_SCALE).astype(BF16)
    ckv = _rms(dkv_ref[...], kvg_ref[...])
    krope = kr_ref[...] * cos + krr_ref[...] * sin
    ckv_ref[...] = ckv
    krope_ref[...] = krope
    kcat_ref[:, :LANE] = ckv.astype(BF16)
    kcat_ref[:, LANE:] = krope.astype(BF16)


def _mlaproj(dq, dkv, kr, krr, pos, inv, qg, kvg, wuq, wuk):
    n = dq.shape[0]
    row = lambda width: pl.BlockSpec((TM, width), lambda i: (i, 0))
    return pl.pallas_call(
        _mlaproj_kernel,
        grid=(n // TM,),
        in_specs=[row(Q_LORA), row(KV_LORA), row(LANE), row(LANE), row(1), _const_spec((1, LANE)),
                  _const_spec((1, Q_LORA)), _const_spec((1, KV_LORA)),
                  _const_spec(wuq.shape), _const_spec(wuk.shape)],
        out_specs=[pl.BlockSpec((MLA_HEADS, TM, 2 * LANE), lambda i: (0, i, 0)),
                   row(KV_LORA), row(LANE), row(2 * LANE)],
        out_shape=[jax.ShapeDtypeStruct((MLA_HEADS, n, 2 * LANE), BF16),
                   jax.ShapeDtypeStruct((n, KV_LORA), F32), jax.ShapeDtypeStruct((n, LANE), F32),
                   jax.ShapeDtypeStruct((n, 2 * LANE), BF16)],
        compiler_params=_params("parallel"),
        name="mlaproj",
    )(dq, dkv, kr, krr, pos, inv, qg, kvg, wuq, wuk)


def _sb_block(z, mask, carry, tri):
    c = z.shape[1]
    t = _softplus_neg_abs(z)
    log_beta = jnp.minimum(z, 0.0) - t
    log_keep = jnp.minimum(-z, 0.0) - t
    if mask is not None:
        log_keep = jnp.where(mask, log_keep, 0.0)
    hi = log_keep.astype(BF16)
    lo = (log_keep - hi.astype(F32)).astype(BF16)
    s = jnp.dot(hi, tri, preferred_element_type=F32) + jnp.dot(lo, tri, preferred_element_type=F32)
    suffix = s[:, :c]
    total = s[:, c:]
    reps = c // LANE
    carry_b = carry if reps == 1 else jnp.concatenate([carry] * reps, axis=1)
    w = jnp.exp(log_beta + suffix + carry_b)
    if mask is not None:
        w = jnp.where(mask, w, 0.0)
    return w, total


def _tri(c):
    r = np.arange(c)
    upper = (r[:, None] > r[None, :]).astype(np.float32)
    return jnp.asarray(np.concatenate([upper, np.ones((c, LANE), np.float32)], axis=1), BF16)


def _sbp_kernel(q_ref, k_ref, v_ref, tri_ref, o_ref, carry_ref, acc_ref):
    i = pl.program_id(1)
    rows = SB_GROUP * QB
    tri = tri_ref[...]
    qpos = lax.broadcasted_iota(jnp.int32, (rows, QB), 0) % QB
    kpos = lax.broadcasted_iota(jnp.int32, (rows, QB), 1)
    diag_mask = kpos < qpos

    for kv in range(SB_KV_HEADS):
        q = q_ref[kv * SB_GROUP:(kv + 1) * SB_GROUP].reshape(rows, LANE)

        def step(j, mask):
            start = pl.multiple_of(j * QB, QB)
            kb = k_ref[pl.ds(start, QB), :]
            vb = v_ref[pl.ds(start, QB), :]
            z = lax.dot_general(q, kb, (((1,), (1,)), ((), ())), preferred_element_type=F32)
            w, total = _sb_block(z, mask, carry_ref[...], tri)
            acc_ref[...] += jnp.dot(w.astype(BF16), vb, preferred_element_type=F32)
            carry_ref[...] += total

        carry_ref[...] = jnp.zeros_like(carry_ref)
        acc_ref[...] = jnp.zeros_like(acc_ref)
        step(i, diag_mask)

        def cond(c):
            j, live = c
            return jnp.logical_and(j >= 0, live > SB_LOG_FLOOR)

        def body(c):
            j, _ = c
            step(j, None)
            return j - 1, jnp.max(carry_ref[...])

        lax.while_loop(cond, body, (i - 1, jnp.max(carry_ref[...])))
        for g in range(SB_GROUP):
            h = kv * SB_GROUP + g
            o_ref[:, h * LANE:(h + 1) * LANE] = acc_ref[g * QB:(g + 1) * QB, :].astype(BF16)


def _sb_prompt(qsb, kb, vb, n_batch, t_pad):
    n = n_batch * t_pad
    nq = t_pad // QB
    rows = SB_GROUP * QB
    return pl.pallas_call(
        _sbp_kernel,
        grid=(n_batch, nq),
        in_specs=[pl.BlockSpec((SB_HEADS, QB, LANE), lambda b, i: (0, b * nq + i, 0)),
                  pl.BlockSpec((t_pad, LANE), lambda b, i: (b, 0)),
                  pl.BlockSpec((t_pad, LANE), lambda b, i: (b, 0)),
                  _const_spec((QB, QB + LANE))],
        out_specs=pl.BlockSpec((QB, SB_HEADS * LANE), lambda b, i: (b * nq + i, 0)),
        out_shape=jax.ShapeDtypeStruct((n, SB_HEADS * LANE), BF16),
        scratch_shapes=[pltpu.VMEM((rows, LANE), F32), pltpu.VMEM((rows, LANE), F32)],
        compiler_params=_params("parallel", "arbitrary"),
        name="sb_prompt",
    )(qsb, kb, vb, _tri(QB))


def _mlap_kernel(q_ref, kc_ref, wuv_ref, o_ref, m_ref, l_ref, acc_ref):
    i = pl.program_id(1)
    rows = MLA_HEADS * QB
    m_ref[...] = jnp.full_like(m_ref, NEG_INF)
    l_ref[...] = jnp.zeros_like(l_ref)
    acc_ref[...] = jnp.zeros_like(acc_ref)
    q = q_ref[...].reshape(rows, 2 * LANE)

    def keys(j):
        return kc_ref[pl.ds(pl.multiple_of(j * MLA_KB, MLA_KB), MLA_KB), :]

    def scores(j):
        return lax.dot_general(q, keys(j), (((1,), (1,)), ((), ())), preferred_element_type=F32)

    def accumulate(j, s):
        parts = [s[:, c * LANE:(c + 1) * LANE] for c in range(MLA_KB // LANE)]
        m_old = m_ref[...]
        m_new = jnp.maximum(m_old, jnp.max(functools.reduce(jnp.maximum, parts), axis=1, keepdims=True))
        alpha = jnp.exp2(m_old - m_new)
        probs = [jnp.exp2(part - m_new) for part in parts]
        l_ref[...] = alpha * l_ref[...] + functools.reduce(jnp.add, probs)
        p = jnp.concatenate([pr.astype(BF16) for pr in probs], axis=1)
        acc_ref[...] = alpha * acc_ref[...] + jnp.dot(p, keys(j)[:, :LANE], preferred_element_type=F32)
        m_ref[...] = m_new

    def body(j, s):
        s_next = scores(j + 1)
        accumulate(j, s)
        return s_next

    last = (i * QB) // MLA_KB
    s = lax.fori_loop(0, last, body, scores(0))
    qpos = i * QB + lax.broadcasted_iota(jnp.int32, (rows, MLA_KB), 0) % QB
    kpos = last * MLA_KB + lax.broadcasted_iota(jnp.int32, (rows, MLA_KB), 1)
    accumulate(last, jnp.where(kpos <= qpos, s, NEG_INF))
    o_lat = (acc_ref[...] / jnp.sum(l_ref[...], axis=1, keepdims=True)).astype(BF16)
    out = jnp.zeros((QB, MLA_HEADS * V_HEAD), F32)
    for h in range(MLA_HEADS):
        out = out + jnp.dot(o_lat[h * QB:(h + 1) * QB], wuv_ref[h], preferred_element_type=F32)
    o_ref[...] = out.astype(BF16)


def _mla_prompt(qcat, kcat, wuv, n_batch, t_pad):
    n = n_batch * t_pad
    nq = t_pad // QB
    rows = MLA_HEADS * QB
    return pl.pallas_call(
        _mlap_kernel,
        grid=(n_batch, nq),
        in_specs=[pl.BlockSpec((MLA_HEADS, QB, 2 * LANE), lambda b, i: (0, b * nq + i, 0)),
                  pl.BlockSpec((t_pad, 2 * LANE), lambda b, i: (b, 0)),
                  _const_spec(wuv.shape)],
        out_specs=pl.BlockSpec((QB, MLA_HEADS * V_HEAD), lambda b, i: (b * nq + i, 0)),
        out_shape=jax.ShapeDtypeStruct((n, MLA_HEADS * V_HEAD), BF16),
        scratch_shapes=[pltpu.VMEM((rows, LANE), F32)] * 3,
        compiler_params=_params("parallel", "arbitrary"),
        name="mla_prompt",
    )(qcat, kcat, wuv)


def _gelu_tanh(x):
    return 0.5 * x * (1.0 + jnp.tanh(math.sqrt(2.0 / math.pi) * (x + 0.044715 * (x * x * x))))


def _lru_coeffs(xc, wg_ref, brg_ref, big_ref, lam_ref, first_is_start):
    w = xc.shape[1]
    gates = jnp.dot(xc.astype(BF16), wg_ref[...], preferred_element_type=F32)
    r = jax.nn.sigmoid(gates[:, :w] + brg_ref[...])
    i = jax.nn.sigmoid(gates[:, w:] + big_ref[...])
    lam = lam_ref[...]
    softplus_neg_lam = jnp.maximum(-lam, 0.0) + _softplus_neg_abs(lam)
    log_a = -LRU_C * r * softplus_neg_lam
    a = jnp.exp(log_a)
    mult = jnp.sqrt(-jnp.tanh(log_a) * (a * a + 1.0))
    if first_is_start is not None:
        mult = jnp.where(first_is_start, 1.0, mult)
    return a, mult * i * xc


def _lrup_kernel(x_ref, g_ref, cw_ref, cb_ref, wg_ref, brg_ref, big_ref, lam_ref, y_ref, hl_ref,
                 xbuf_ref, h_ref, *, last_row):
    c = pl.program_id(1)
    tc = x_ref.shape[0]
    tail = CONV_WIDTH - 1

    @pl.when(c == 0)
    def _():
        xbuf_ref[:SUBLANE] = jnp.zeros((SUBLANE, xbuf_ref.shape[1]), F32)
        h_ref[...] = jnp.zeros_like(h_ref)

    x = x_ref[...]
    xbuf_ref[SUBLANE:] = x
    xc = cb_ref[...] + xbuf_ref[SUBLANE - tail:SUBLANE - tail + tc] * cw_ref[0:1]
    for j in range(1, CONV_WIDTH):
        xc = xc + xbuf_ref[SUBLANE - tail + j:SUBLANE - tail + j + tc] * cw_ref[j:j + 1]
    xbuf_ref[:SUBLANE] = x[tc - SUBLANE:]

    row = lax.broadcasted_iota(jnp.int32, (tc, 1), 0)
    a, b = _lru_coeffs(xc, wg_ref, brg_ref, big_ref, lam_ref, jnp.logical_and(row == 0, c == 0))
    shift = 1
    while shift < tc:
        a_prev = pltpu.roll(a, shift, 0)
        b_prev = pltpu.roll(b, shift, 0)
        ok = row >= shift
        b = jnp.where(ok, a * b_prev + b, b)
        a = jnp.where(ok, a * a_prev, a)
        shift *= 2
    h = a * h_ref[SUBLANE - 1:SUBLANE] + b
    h_ref[...] = h[tc - SUBLANE:]
    y_ref[...] = (h * _gelu_tanh(g_ref[...])).astype(BF16)

    lc, lr = divmod(last_row, tc)
    base = (lr // SUBLANE) * SUBLANE

    @pl.when(c == lc)
    def _():
        hl_ref[0] = h[base:base + SUBLANE]


def _lru_prompt(lx, lg, cw, cb, wg, brg, big, lam, n_batch, t_pad, t_real):
    n, w = n_batch * t_pad, lx.shape[1]
    nc = t_pad // LRU_CHUNK
    row = pl.BlockSpec((LRU_CHUNK, w), lambda b, c: (b * nc + c, 0))
    y, hl = pl.pallas_call(
        functools.partial(_lrup_kernel, last_row=t_real - 1),
        grid=(n_batch, nc),
        in_specs=[row, row, _const_spec(cw.shape), _const_spec(cb.shape), _const_spec(wg.shape),
                  _const_spec(brg.shape), _const_spec(big.shape), _const_spec(lam.shape)],
        out_specs=[row, pl.BlockSpec((1, SUBLANE, w), lambda b, c: (b, 0, 0))],
        out_shape=[jax.ShapeDtypeStruct((n, w), BF16), jax.ShapeDtypeStruct((n_batch, SUBLANE, w), F32)],
        scratch_shapes=[pltpu.VMEM((LRU_CHUNK + SUBLANE, w), F32), pltpu.VMEM((SUBLANE, w), F32)],
        compiler_params=_params("parallel", "arbitrary"),
        name="lru_prompt",
    )(lx, lg, cw, cb, wg, brg, big, lam)
    return y, hl[:, (t_real - 1) % SUBLANE]


def _lrus_kernel(x_ref, g_ref, conv0_ref, h0_ref, cw_ref, cb_ref, wg_ref, brg_ref, big_ref, lam_ref,
                 y_ref, hn_ref, *, first_pos):
    steps = x_ref.shape[0]
    tail = CONV_WIDTH - 1
    xin = [conv0_ref[j] for j in range(tail)] + [x_ref[t] for t in range(steps)]
    h = h0_ref[...]
    for t in range(steps):
        xc = cb_ref[...] + xin[t] * cw_ref[0:1]
        for j in range(1, CONV_WIDTH):
            xc = xc + xin[t + j] * cw_ref[j:j + 1]
        start = None
        if first_pos + t == 0:
            start = jnp.full((xc.shape[0], 1), True)
        a, b = _lru_coeffs(xc, wg_ref, brg_ref, big_ref, lam_ref, start)
        h = a * h + b
        y_ref[t] = (h * _gelu_tanh(g_ref[t])).astype(BF16)
    hn_ref[...] = h


def _lru_sample(lx_t, lg_t, conv0_t, h0, cw, cb, wg, brg, big, lam, first_pos):
    steps, nb, w = lx_t.shape
    args = (lx_t, lg_t, conv0_t, h0, cw, cb, wg, brg, big, lam)
    return pl.pallas_call(
        functools.partial(_lrus_kernel, first_pos=first_pos),
        grid=(1,),
        in_specs=[_const_spec(a.shape) for a in args],
        out_specs=[_const_spec((steps, nb, w)), _const_spec((nb, w))],
        out_shape=[jax.ShapeDtypeStruct((steps, nb, w), BF16), jax.ShapeDtypeStruct((nb, w), F32)],
        compiler_params=_params("arbitrary"),
        name="lru_sample",
    )(*args)


def _layer_norm(x, g, b):
    mu = jnp.mean(x, axis=-1, keepdims=True)
    xc = x - mu
    var = jnp.mean(jnp.square(xc), axis=-1, keepdims=True)
    return xc * lax.rsqrt(var + LN_EPS) * g + b


def _route(sel, s):
    e_idx = lax.broadcasted_iota(jnp.int32, sel.shape, 0)
    rows = [sel[e:e + 1] for e in range(N_EXPERTS)]
    best_score = None
    best = None
    for g in range(N_GROUPS):
        v = rows[g * EXPERTS_PER_GROUP:(g + 1) * EXPERTS_PER_GROUP]
        top2 = None
        for a in range(EXPERTS_PER_GROUP):
            for b in range(a + 1, EXPERTS_PER_GROUP):
                pair = v[a] + v[b]
                top2 = pair if top2 is None else jnp.maximum(top2, pair)
        if g == 0:
            best_score, best = top2, jnp.zeros(top2.shape, jnp.int32)
        else:
            better = top2 > best_score
            best = jnp.where(better, g, best)
            best_score = jnp.where(better, top2, best_score)
    masked = jnp.where(e_idx // EXPERTS_PER_GROUP == best, sel, NEG_INF)
    m1 = jnp.max(masked, axis=0, keepdims=True)
    i1 = jnp.min(jnp.where(masked == m1, e_idx, N_EXPERTS), axis=0, keepdims=True)
    rest = jnp.where(e_idx == i1, -jnp.inf, masked)
    m2 = jnp.max(rest, axis=0, keepdims=True)
    i2 = jnp.min(jnp.where(rest == m2, e_idx, N_EXPERTS), axis=0, keepdims=True)
    pick1 = e_idx == i1
    pick2 = e_idx == i2
    w1 = jnp.sum(jnp.where(pick1, s, 0.0), axis=0, keepdims=True)
    w2 = jnp.sum(jnp.where(pick2, s, 0.0), axis=0, keepdims=True)
    denom = w1 + w2
    return jnp.where(pick1, w1 / denom, 0.0) + jnp.where(pick2, w2 / denom, 0.0)


def _merge_kernel(x_ref, osb_p_ref, olru_p_ref, omla_p_ref, osb_s_ref, olru_s_ref, omla_s_ref, gates_ref, bg_ref,
                  wb0_ref, wb1_ref, wb2_ref, wo_ref, g1_ref, b1_ref, wr_ref, rb_ref, x1_ref, gate_ref,
                  *, alpha, prompt_tiles):
    d = x_ref.shape[1]
    is_prompt = pl.program_id(0) < prompt_tiles
    branches = ((osb_p_ref, osb_s_ref, wb0_ref), (olru_p_ref, olru_s_ref, wb1_ref), (omla_p_ref, omla_s_ref, wb2_ref))
    mixed = None
    for i, (op_ref, os_ref, w_ref) in enumerate(branches):
        sl = slice(i * d, (i + 1) * d)
        g = jax.nn.sigmoid(gates_ref[:, sl] + bg_ref[:, sl])
        o = jnp.where(is_prompt, op_ref[...], os_ref[...])
        term = g * jnp.dot(o, w_ref[...], preferred_element_type=F32)
        mixed = term if mixed is None else mixed + term
    mix = jnp.dot(mixed.astype(BF16), wo_ref[...], preferred_element_type=F32)
    x1 = _layer_norm(alpha * x_ref[...] + mix, g1_ref[...], b1_ref[...])
    x1_ref[...] = x1
    logits = lax.dot_general(wr_ref[...], x1, (((1,), (1,)), ((), ())),
                             precision=lax.Precision.HIGHEST, preferred_element_type=F32)
    s = jax.nn.sigmoid(logits)
    gate_ref[...] = _route(s + rb_ref[...], s)


def _merge(x, prompt_outs, sample_outs, gates, bg, wb0, wb1, wb2, wo, g1, b1, wr, rb, alpha):
    n, d = x.shape
    prompt_tiles = prompt_outs[0].shape[0] // TM
    assert all(o.shape[0] == prompt_tiles * TM for o in prompt_outs)
    assert all(o.shape[0] == n - prompt_tiles * TM for o in sample_outs)
    row = lambda width: pl.BlockSpec((TM, width), lambda i: (i, 0))
    prow = lambda width: pl.BlockSpec((TM, width), lambda i: (jnp.minimum(i, prompt_tiles - 1), 0))
    srow = lambda width: pl.BlockSpec((TM, width), lambda i: (jnp.maximum(i - prompt_tiles, 0), 0))
    consts = (bg, wb0, wb1, wb2, wo, g1, b1, wr, rb)
    return pl.pallas_call(
        functools.partial(_merge_kernel, alpha=alpha, prompt_tiles=prompt_tiles),
        grid=(n // TM,),
        in_specs=[row(d)] + [prow(o.shape[1]) for o in prompt_outs] + [srow(o.shape[1]) for o in sample_outs]
        + [row(gates.shape[1])] + [_const_spec(c.shape) for c in consts],
        out_specs=[row(d), pl.BlockSpec((N_EXPERTS, TM), lambda i: (0, i))],
        out_shape=[jax.ShapeDtypeStruct((n, d), F32), jax.ShapeDtypeStruct((N_EXPERTS, n), F32)],
        compiler_params=_params("parallel"),
        name="merge",
    )(x, *prompt_outs, *sample_outs, gates, *consts)


def _moe_kernel(x_ref, gate_ref, wg_ref, wu_ref, wd_ref, g2_ref, b2_ref, o_ref, xb_ref, acc_ref, *, alpha):
    e = pl.program_id(1)

    @pl.when(e == 0)
    def _():
        xb_ref[...] = x_ref[...].astype(BF16)
        acc_ref[...] = jnp.zeros_like(acc_ref)

    xb = xb_ref[...]
    hg = jnp.dot(xb, wg_ref[0, 0], preferred_element_type=F32)
    hu = jnp.dot(xb, wu_ref[0, 0], preferred_element_type=F32)
    gate = gate_ref[...]
    lane = lax.broadcasted_iota(jnp.int32, gate.shape, 1)
    gcol = jnp.sum(jnp.where(lane == e, gate, 0.0), axis=1, keepdims=True)
    h = (hg * jax.nn.sigmoid(hg)) * hu * gcol
    acc_ref[...] += jnp.dot(h.astype(BF16), wd_ref[0, 0], preferred_element_type=F32)

    @pl.when(e == pl.num_programs(1) - 1)
    def _():
        o_ref[...] = _layer_norm(alpha * x_ref[...] + acc_ref[...], g2_ref[...], b2_ref[...])


def _moe(x, gate, wg, wu, wd, layer, g2, b2, alpha):
    n, d = x.shape
    f = wg.shape[-1]
    return pl.pallas_call(
        functools.partial(_moe_kernel, alpha=alpha),
        grid=(n // TM_MOE, N_EXPERTS),
        in_specs=[pl.BlockSpec((TM_MOE, d), lambda i, e: (i, 0)),
                  pl.BlockSpec((TM_MOE, N_EXPERTS), lambda i, e: (i, 0)),
                  pl.BlockSpec((1, 1, d, f), lambda i, e: (layer, e, 0, 0)),
                  pl.BlockSpec((1, 1, d, f), lambda i, e: (layer, e, 0, 0)),
                  pl.BlockSpec((1, 1, f, d), lambda i, e: (layer, e, 0, 0)),
                  _const_spec(g2.shape), _const_spec(b2.shape)],
        out_specs=pl.BlockSpec((TM_MOE, d), lambda i, e: (i, 0)),
        out_shape=jax.ShapeDtypeStruct((n, d), F32),
        scratch_shapes=[pltpu.VMEM((TM_MOE, d), BF16), pltpu.VMEM((TM_MOE, d), F32)],
        compiler_params=_params("parallel", "arbitrary"),
        name="moe",
    )(x, gate, wg, wu, wd, g2, b2)


def _seq_copies(pt_ref, seq, slot, layer, n_pages, page, kc_ref, vc_ref, cc_ref, rc_ref,
                ktop_ref, vtop_ref, cb_ref, rt_ref, sem_ref):
    copies = []
    for j in range(n_pages):
        pid = pt_ref[seq, j]
        copies.append(pltpu.make_async_copy(cc_ref.at[layer, pid], cb_ref.at[slot, pl.ds(j * page, page), :],
                                            sem_ref.at[slot, 0]))
        copies.append(pltpu.make_async_copy(rc_ref.at[layer, pid], rt_ref.at[slot, :, pl.ds(j * page, page)],
                                            sem_ref.at[slot, 1]))
    per_chunk = SB_CHUNK // page
    for u in range(per_chunk):
        pid = pt_ref[seq, n_pages - per_chunk + u]
        lanes = pl.ds(u * page, page)
        copies.append(pltpu.make_async_copy(kc_ref.at[layer, pid], ktop_ref.at[slot, :, :, lanes], sem_ref.at[slot, 2]))
        copies.append(pltpu.make_async_copy(vc_ref.at[layer, pid], vtop_ref.at[slot, :, :, lanes], sem_ref.at[slot, 3]))
    return copies


def _chunk_copies(pt_ref, seq, chunk, par, layer, page, kc_ref, vc_ref, kold_ref, vold_ref, sem_ref):
    copies = []
    per_chunk = SB_CHUNK // page
    for u in range(per_chunk):
        pid = pt_ref[seq, chunk * per_chunk + u]
        lanes = pl.ds(u * page, page)
        copies.append(pltpu.make_async_copy(kc_ref.at[layer, pid], kold_ref.at[par, :, :, lanes], sem_ref.at[par]))
        copies.append(pltpu.make_async_copy(vc_ref.at[layer, pid], vold_ref.at[par, :, :, lanes], sem_ref.at[par]))
    return copies


def _samp_kernel(pt_ref, qsb_ref, knew_ref, vnew_ref, qcat_ref, kcnew_ref, tri_ref, trin_ref, wuv_ref,
                 kc_ref, vc_ref, cc_ref, rc_ref, osb_ref, omla_ref,
                 ktop_ref, vtop_ref, cb_ref, rt_ref, kold_ref, vold_ref, sem_ref, osem_ref, carry_ref, acc_ref,
                 *, layer, n_pages, page, steps):
    s = pl.program_id(0)
    ns = pl.num_programs(0)
    slot = s % 2
    past = n_pages * page
    ahead = (ktop_ref, vtop_ref, cb_ref, rt_ref, sem_ref)
    caches = (kc_ref, vc_ref, cc_ref, rc_ref)

    @pl.when(s == 0)
    def _():
        for cp in _seq_copies(pt_ref, 0, 0, layer, n_pages, page, *caches, *ahead):
            cp.start()

    @pl.when(s + 1 < ns)
    def _():
        for cp in _seq_copies(pt_ref, s + 1, 1 - slot, layer, n_pages, page, *caches, *ahead):
            cp.start()

    for cp in _seq_copies(pt_ref, s, slot, layer, n_pages, page, *caches, *ahead):
        cp.wait()

    rows = SB_GROUP * SUBLANE
    tq = lax.broadcasted_iota(jnp.int32, (rows, LANE), 0) % SUBLANE
    tk = lax.broadcasted_iota(jnp.int32, (rows, LANE), 1)
    new_mask = tk < tq
    tri = tri_ref[...]
    n_chunks = past // SB_CHUNK

    def past_chunk(kv, qk, k_ref, v_ref, buf):
        kt = k_ref[buf, kv].astype(BF16)
        vt = v_ref[buf, kv].astype(BF16)
        z = jnp.dot(qk, kt, preferred_element_type=F32)
        w, total = _sb_block(z, None, carry_ref[kv], tri)
        pv = lax.dot_general(w.astype(BF16), vt, (((1,), (1,)), ((), ())), preferred_element_type=F32)
        acc_ref[kv, :, kv * SB_HEAD_DIM:(kv + 1) * SB_HEAD_DIM] += pv
        carry_ref[kv] += total

    qks = []
    for kv in range(SB_KV_HEADS):
        q = qsb_ref[0, kv * SB_GROUP:(kv + 1) * SB_GROUP].reshape(rows, LANE).astype(BF16)
        z = lax.dot_general(q, knew_ref[0], (((1,), (1,)), ((), ())), preferred_element_type=F32)
        w, total = _sb_block(z, new_mask, jnp.zeros((rows, LANE), F32), trin_ref[...])
        acc_ref[kv] = jnp.dot(w.astype(BF16), vnew_ref[0], preferred_element_type=F32)
        carry_ref[kv] = total
        qks.append(q[:, kv * SB_HEAD_DIM:(kv + 1) * SB_HEAD_DIM])
        past_chunk(kv, qks[kv], ktop_ref, vtop_ref, slot)

    mrows = MLA_HEADS * SUBLANE
    q = qcat_ref[0].reshape(mrows, 2 * LANE).astype(BF16)
    cb = cb_ref[slot].astype(BF16)
    rt = rt_ref[slot].astype(BF16)
    s_past = lax.dot_general(q[:, :LANE], cb, (((1,), (1,)), ((), ())), preferred_element_type=F32)
    s_past = s_past + jnp.dot(q[:, LANE:LANE + QK_ROPE], rt, preferred_element_type=F32)
    kcn = kcnew_ref[0]
    s_new = lax.dot_general(q, kcn, (((1,), (1,)), ((), ())), preferred_element_type=F32)
    tq = lax.broadcasted_iota(jnp.int32, (mrows, LANE), 0) % SUBLANE
    tk = lax.broadcasted_iota(jnp.int32, (mrows, LANE), 1)
    s_new = jnp.where(jnp.logical_and(tk <= tq, tk < steps), s_new, NEG_INF)
    m = jnp.maximum(jnp.max(s_past, axis=1, keepdims=True), jnp.max(s_new, axis=1, keepdims=True))
    p_past = jnp.exp2(s_past - m)
    p_new = jnp.exp2(s_new - m)
    denom = jnp.sum(p_past, axis=1, keepdims=True) + jnp.sum(p_new, axis=1, keepdims=True)
    o_lat = jnp.dot(p_past.astype(BF16), cb, preferred_element_type=F32)
    o_lat = o_lat + jnp.dot(p_new.astype(BF16), kcn[:, :LANE], preferred_element_type=F32)
    o_lat = o_lat / denom
    out = jnp.zeros((SUBLANE, MLA_HEADS * V_HEAD), F32)
    for h in range(MLA_HEADS):
        out = out + jnp.dot(o_lat[h * SUBLANE:(h + 1) * SUBLANE].astype(BF16), wuv_ref[h],
                            preferred_element_type=F32)
    omla_ref[0] = out

    if n_chunks >= 2:
        old = (kc_ref, vc_ref, kold_ref, vold_ref, osem_ref)
        first = n_chunks - 2
        live0 = jnp.max(carry_ref[...])

        @pl.when(live0 > SB_LOG_FLOOR)
        def _():
            for cp in _chunk_copies(pt_ref, s, first, first % 2, layer, page, *old):
                cp.start()

        def cond(c):
            j, live = c
            return jnp.logical_and(j >= 0, live > SB_LOG_FLOOR)

        def body(c):
            j, _ = c
            par = j % 2
            for cp in _chunk_copies(pt_ref, s, j, par, layer, page, *old):
                cp.wait()

            @pl.when(j >= 1)
            def _():
                for cp in _chunk_copies(pt_ref, s, j - 1, 1 - par, layer, page, *old):
                    cp.start()

            for kv in range(SB_KV_HEADS):
                past_chunk(kv, qks[kv], kold_ref, vold_ref, par)
            return j - 1, jnp.max(carry_ref[...])

        j_end, _ = lax.while_loop(cond, body, (first, live0))

        @pl.when(jnp.logical_and(live0 > SB_LOG_FLOOR, j_end >= 0))
        def _():
            for cp in _chunk_copies(pt_ref, s, j_end, j_end % 2, layer, page, *old):
                cp.wait()

    for kv in range(SB_KV_HEADS):
        osb_ref[0, kv] = acc_ref[kv]


def _sample_attn(page_table, qsb_s, knew, vnew, qcat_s, kcnew, wuv, kc, vc, cc, rc, layer, steps):
    nseq, n_pages = page_table.shape
    page = cc.shape[2]
    past = n_pages * page
    assert past % SB_CHUNK == 0 and SB_CHUNK % page == 0
    rows = SB_GROUP * SUBLANE
    any_spec = pl.BlockSpec(memory_space=pl.ANY)
    seq4 = lambda a, b, c: pl.BlockSpec((1, a, b, c), lambda s, pt: (s, 0, 0, 0))
    seq3 = lambda a, b: pl.BlockSpec((1, a, b), lambda s, pt: (s, 0, 0))
    const = lambda shape: pl.BlockSpec(shape, lambda s, pt: (0,) * len(shape))
    grid_spec = pltpu.PrefetchScalarGridSpec(
        num_scalar_prefetch=1,
        grid=(nseq,),
        in_specs=[seq4(SB_HEADS, SUBLANE, LANE), seq3(LANE, LANE), seq3(LANE, LANE),
                  seq4(MLA_HEADS, SUBLANE, 2 * LANE), seq3(LANE, 2 * LANE),
                  const((SB_CHUNK, SB_CHUNK + LANE)), const((LANE, 2 * LANE)), const(wuv.shape),
                  any_spec, any_spec, any_spec, any_spec],
        out_specs=[seq4(SB_KV_HEADS, rows, LANE), seq3(SUBLANE, MLA_HEADS * V_HEAD)],
        scratch_shapes=[pltpu.VMEM((2, SB_KV_HEADS, SB_HEAD_DIM, SB_CHUNK), F32),
                        pltpu.VMEM((2, SB_KV_HEADS, SB_HEAD_DIM, SB_CHUNK), F32),
                        pltpu.VMEM((2, past, KV_LORA), F32),
                        pltpu.VMEM((2, QK_ROPE, past), F32),
                        pltpu.VMEM((2, SB_KV_HEADS, SB_HEAD_DIM, SB_CHUNK), F32),
                        pltpu.VMEM((2, SB_KV_HEADS, SB_HEAD_DIM, SB_CHUNK), F32),
                        pltpu.SemaphoreType.DMA((2, 4)),
                        pltpu.SemaphoreType.DMA((2,)),
                        pltpu.VMEM((SB_KV_HEADS, rows, LANE), F32), pltpu.VMEM((SB_KV_HEADS, rows, LANE), F32)])
    return pl.pallas_call(
        functools.partial(_samp_kernel, layer=layer, n_pages=n_pages, page=page, steps=steps),
        grid_spec=grid_spec,
        out_shape=[jax.ShapeDtypeStruct((nseq, SB_KV_HEADS, rows, LANE), F32),
                   jax.ShapeDtypeStruct((nseq, SUBLANE, MLA_HEADS * V_HEAD), F32)],
        compiler_params=_params("arbitrary"),
        name="sample_attn",
    )(page_table, qsb_s, knew, vnew, qcat_s, kcnew, _tri(SB_CHUNK), _tri(LANE), wuv, kc, vc, cc, rc)


def _pad_last(a, width):
    return jnp.pad(a, [(0, 0)] * (a.ndim - 1) + [(0, width - a.shape[-1])])


def _rot_half(w):
    half = w.shape[-1] // 2
    return jnp.concatenate([-w[..., half:], w[..., :half]], axis=-1)


def _pack_w_in(w_in):
    depth, d, _ = w_in.shape
    widths = (SB_HEADS * SB_HEAD_DIM, SB_KV_HEADS * SB_HEAD_DIM, SB_KV_HEADS * SB_HEAD_DIM, LRU_WIDTH, LRU_WIDTH,
              Q_LORA, KV_LORA, QK_ROPE, N_BRANCH * d)
    offs = np.concatenate([[0], np.cumsum(widths)])
    wq, wk, wv, wlx, wlg, wdq, wdkv, wkr, wgt = [w_in[..., offs[i]:offs[i + 1]] for i in range(len(widths))]
    wq = wq.reshape(depth, d, SB_HEADS, SB_HEAD_DIM) * (SB_HEAD_DIM ** -0.5)
    zero = jnp.zeros_like(wq)
    first_kv = (jnp.arange(SB_HEADS) // SB_GROUP == 0)[None, None, :, None]
    wq = jnp.where(first_kv, jnp.concatenate([wq, zero], -1), jnp.concatenate([zero, wq], -1))
    wq = wq.reshape(depth, d, SB_HEADS * LANE)
    packed = jnp.concatenate([wq, wk, wv, wlx, wlg, wdq, wdkv, _pad_last(wkr, LANE), _pad_last(_rot_half(wkr), LANE), wgt],
                             axis=-1)
    return packed.astype(BF16)


def _pack_w_uq(w_uq):
    depth = w_uq.shape[0]
    w = w_uq.reshape(depth, Q_LORA, MLA_HEADS, QK_NOPE + QK_ROPE)
    nope = w[..., :QK_NOPE].reshape(depth, Q_LORA, MLA_HEADS * QK_NOPE)
    rope = w[..., QK_NOPE:]
    rope_p = _pad_last(rope, LANE).reshape(depth, Q_LORA, MLA_HEADS * LANE)
    rot_p = _pad_last(_rot_half(rope), LANE).reshape(depth, Q_LORA, MLA_HEADS * LANE)
    return jnp.concatenate([nope, rope_p, rot_p], axis=-1).astype(BF16)


def _pack_w_ukv(w_ukv):
    depth = w_ukv.shape[0]
    w = w_ukv.reshape(depth, KV_LORA, MLA_HEADS, QK_NOPE + V_HEAD)
    eye = jnp.eye(MLA_HEADS, dtype=w.dtype)
    w_uk_t = jnp.transpose(w[..., :QK_NOPE], (0, 2, 3, 1))
    wuk_bd = w_uk_t[:, :, :, None, :] * eye[None, :, None, :, None]
    wuk_bd = wuk_bd.reshape(depth, MLA_HEADS * QK_NOPE, MLA_HEADS * KV_LORA)
    w_uv = jnp.transpose(w[..., QK_NOPE:], (0, 2, 1, 3))
    wuv_p = w_uv[:, :, :, None, :] * eye[None, :, None, :, None]
    wuv_p = wuv_p.reshape(depth, MLA_HEADS, KV_LORA, MLA_HEADS * V_HEAD)
    return wuk_bd.astype(BF16), wuv_p.astype(BF16)


def _pack_lru_gates(w_rg, w_ig):
    depth = w_rg.shape[0]
    bd = w_rg.shape[-1]
    eye = jnp.eye(LRU_BLOCKS, dtype=w_rg.dtype)

    def block_diag(w):
        full = w[:, :, :, None, :] * eye[None, :, None, :, None]
        return full.reshape(depth, LRU_BLOCKS * bd, LRU_BLOCKS * bd)

    return jnp.concatenate([block_diag(w_rg), block_diag(w_ig)], axis=-1).astype(BF16)


def _pack_w_branch0(w):
    depth, _, d = w.shape
    w = w.reshape(depth, SB_HEADS, SB_HEAD_DIM, d)
    zero = jnp.zeros_like(w)
    first_kv = (jnp.arange(SB_HEADS) // SB_GROUP == 0)[None, :, None, None]
    w = jnp.where(first_kv, jnp.concatenate([w, zero], 2), jnp.concatenate([zero, w], 2))
    return w.reshape(depth, SB_HEADS * LANE, d).astype(BF16)


@jax.jit
def kernel(x_prompt, x_sample, cache_sb_k, cache_sb_v, cache_mla_ckv, cache_mla_krope, state_lru_h, state_lru_conv,
           page_table, meta_tokens, w_in, b_gate, conv_w, conv_b, w_rg, b_rg, w_ig, b_ig, lru_lambda, q_norm_g, w_uq,
           kv_norm_g, w_ukv, w_branch, w_out, ln1_g, ln1_b, ln2_g, ln2_b, w_router, router_bias, w_exp_gate,
           w_exp_up, w_exp_down):
    nb, seq, d = x_prompt.shape
    n_meta = meta_tokens.shape[0]
    t_real = n_meta + seq
    t_pad = -(-t_real // MLA_KB) * MLA_KB
    db, steps, _ = x_sample.shape
    depth = w_in.shape[0]
    n_pages = page_table.shape[1]
    page = cache_mla_ckv.shape[2]
    past = n_pages * page
    n_prompt = nb * t_pad
    n_samp = db * steps
    n_tok = -(-(n_prompt + n_samp) // ROW_ALIGN) * ROW_ALIGN
    alpha = (2 * depth) ** 0.25
    assert steps <= SUBLANE and t_pad % LRU_CHUNK == 0

    xp = jnp.concatenate([jnp.broadcast_to(meta_tokens[None], (nb, n_meta, d)), x_prompt,
                          jnp.zeros((nb, t_pad - t_real, d), F32)], axis=1).reshape(n_prompt, d)
    x = jnp.concatenate([xp, x_sample.reshape(n_samp, d), jnp.zeros((n_tok - n_prompt - n_samp, d), F32)], axis=0)
    pos_p = jnp.tile(jnp.arange(t_pad, dtype=F32), nb)
    pos_s = jnp.tile(past + jnp.arange(steps, dtype=F32), db)
    pos = jnp.concatenate([pos_p, pos_s, jnp.zeros((n_tok - n_prompt - n_samp,), F32)])[:, None]
    half = QK_ROPE // 2
    inv = ROPE_THETA ** (-jnp.arange(half, dtype=F32) / half)
    inv = jnp.tile(inv, LANE // half)[None]

    w_in_p = _pack_w_in(w_in)
    wuq_p = _pack_w_uq(w_uq)
    wuk_bd, wuv_p = _pack_w_ukv(w_ukv)
    wgates = _pack_lru_gates(w_rg, w_ig)
    wb0 = _pack_w_branch0(w_branch[:, 0])
    wb1 = w_branch[:, 1].astype(BF16)
    wb2 = w_branch[:, 2].astype(BF16)
    wo = w_out.astype(BF16)
    weg = w_exp_gate.astype(BF16)
    weu = w_exp_up.astype(BF16)
    wed = w_exp_down.astype(BF16)
    wr_t = jnp.transpose(w_router)
    rb = router_bias[:, None]
    kc = jnp.transpose(cache_sb_k, (0, 1, 3, 4, 2))
    vc = jnp.transpose(cache_sb_v, (0, 1, 3, 4, 2))
    rc = jnp.transpose(cache_mla_krope, (0, 1, 3, 2))
    conv0_t = jnp.transpose(state_lru_conv, (0, 2, 1, 3))

    def samp_rows(a):
        return a[..., n_prompt:n_prompt + n_samp, :]

    def per_seq_heads(a):
        h, _, c = a.shape
        a = a.reshape(h, db, steps, c)
        a = jnp.pad(a, ((0, 0), (0, 0), (0, SUBLANE - steps), (0, 0)))
        return jnp.transpose(a, (1, 0, 2, 3)).astype(F32)

    def per_seq_keys(a):
        a = a.reshape(db, steps, a.shape[-1])
        return jnp.pad(a, ((0, 0), (0, LANE - steps), (0, 0)))

    st_p = []
    st_s = []
    for l in range(depth):
        (qsb, k, v, kb, vb, lx, lg, dq, dkv, kr, krr, gates) = _inproj(x, w_in_p[l])
        qcat, ckv, krope, kcat = _mlaproj(dq, dkv, kr, krr, pos, inv, q_norm_g[l][None], kv_norm_g[l][None],
                                          wuq_p[l], wuk_bd[l])
        lru_w = (conv_w[l], conv_b[l][None], wgates[l], b_rg[l][None], b_ig[l][None], lru_lambda[l][None])

        osb = _sb_prompt(qsb, kb, vb, nb, t_pad)
        omla = _mla_prompt(qcat, kcat, wuv_p[l], nb, t_pad)
        olru, h_p = _lru_prompt(lx, lg, *lru_w, nb, t_pad, t_real)

        osb_s, omla_s = _sample_attn(page_table, per_seq_heads(samp_rows(qsb)), per_seq_keys(samp_rows(kb)),
                                     per_seq_keys(samp_rows(vb)), per_seq_heads(samp_rows(qcat)),
                                     per_seq_keys(samp_rows(kcat)), wuv_p[l], kc, vc, cache_mla_ckv, rc, l, steps)
        lx_s = samp_rows(lx).reshape(db, steps, LRU_WIDTH)
        lx_t = jnp.transpose(lx_s, (1, 0, 2))
        lg_t = jnp.transpose(samp_rows(lg).reshape(db, steps, LRU_WIDTH), (1, 0, 2))
        olru_t, h_s = _lru_sample(lx_t, lg_t, conv0_t[l], state_lru_h[l], *lru_w, past)

        osb_s = osb_s.reshape(db, SB_KV_HEADS, SB_GROUP, SUBLANE, LANE)[:, :, :, :steps]
        osb_s = jnp.transpose(osb_s, (0, 3, 1, 2, 4)).reshape(n_samp, SB_HEADS * LANE).astype(BF16)
        omla_s = omla_s[:, :steps].reshape(n_samp, MLA_HEADS * V_HEAD).astype(BF16)
        olru_s = jnp.transpose(olru_t, (1, 0, 2)).reshape(n_samp, LRU_WIDTH)
        tail_pad = ((0, n_tok - n_prompt - n_samp), (0, 0))
        sample_outs = [jnp.pad(o, tail_pad) for o in (osb_s, olru_s, omla_s)]

        x1, gate_t = _merge(x, (osb, olru, omla), sample_outs, gates, b_gate[l][None], wb0[l], wb1[l], wb2[l],
                            wo[l], ln1_g[l][None], ln1_b[l][None], wr_t, rb, alpha)
        x = _moe(x1, jnp.transpose(gate_t), weg, weu, wed, l, ln2_g[l][None], ln2_b[l][None], alpha)

        def prompt_state(a, width):
            return a[:n_prompt].reshape(nb, t_pad, -1)[:, :t_real, :width]

        conv_in_s = jnp.concatenate([state_lru_conv[l], lx_s], axis=1)
        st_p.append((prompt_state(k, LANE).reshape(nb, t_real, SB_KV_HEADS, SB_HEAD_DIM),
                     prompt_state(v, LANE).reshape(nb, t_real, SB_KV_HEADS, SB_HEAD_DIM),
                     prompt_state(ckv, KV_LORA), prompt_state(krope, QK_ROPE), h_p,
                     prompt_state(lx, LRU_WIDTH)[:, t_real - (CONV_WIDTH - 1):]))
        st_s.append((samp_rows(k).reshape(db, steps, SB_KV_HEADS, SB_HEAD_DIM),
                     samp_rows(v).reshape(db, steps, SB_KV_HEADS, SB_HEAD_DIM),
                     samp_rows(ckv).reshape(db, steps, KV_LORA),
                     samp_rows(krope)[:, :QK_ROPE].reshape(db, steps, QK_ROPE), h_s,
                     conv_in_s[:, steps:]))

    p_state = [jnp.stack(f, axis=0) for f in zip(*st_p)]
    s_state = [jnp.stack(f, axis=0) for f in zip(*st_s)]
    y_prompt = x[:n_prompt].reshape(nb, t_pad, d)[:, n_meta:t_real]
    y_sample = x[n_prompt:n_prompt + n_samp].reshape(db, steps, d)
    return (y_prompt, y_sample, *p_state, *s_state)
```

```python
import functools
import math

import jax
import jax.numpy as jnp
import numpy as np
from jax import lax
from jax.experimental import pallas as pl
from jax.experimental.pallas import tpu as pltpu

F32 = jnp.float32
BF16 = jnp.bfloat16

SB_HEADS = 8
SB_KV_HEADS = 2
SB_GROUP = SB_HEADS // SB_KV_HEADS
SB_HEAD_DIM = 64
LRU_WIDTH = 512
LRU_BLOCKS = 8
CONV_WIDTH = 4
LRU_C = 8.0
MLA_HEADS = 8
Q_LORA = 256
KV_LORA = 128
QK_NOPE = 64
QK_ROPE = 32
V_HEAD = 64
MLA_SCALE = (QK_NOPE + QK_ROPE) ** -0.5
MLA_Q_SCALE = MLA_SCALE * math.log2(math.e)
ROPE_THETA = 10000.0
N_BRANCH = 3
BRANCH_WIDTH = 512
N_EXPERTS = 16
N_GROUPS = 4
EXPERTS_PER_GROUP = N_EXPERTS // N_GROUPS
D_EXPERT = 256
LN_EPS = 1e-5
RMS_EPS = 1e-6
NEG_INF = -1e30

LANE = 128
SUBLANE = 8
QB = 128
MLA_KB = 256
TM = 512
TM_MOE_CHOICES = (1024, 896, 768, 640, 512)
MOE_EXPERTS_PER_STEP = 4
ROW_ALIGN = 512
LRU_CHUNK = 128
SB_CHUNK = 256
SB_LOG_FLOOR = -40.0
VMEM_LIMIT = 56 * 1024 * 1024

C_Q = 0
C_K = C_Q + SB_HEADS * LANE
C_V = C_K + LANE
C_LX = C_V + LANE
C_LG = C_LX + LRU_WIDTH
C_DQ = C_LG + LRU_WIDTH
C_DKV = C_DQ + Q_LORA
C_KR = C_DKV + KV_LORA
C_KRR = C_KR + LANE
C_G = C_KRR + LANE


def _const_spec(shape):
    zeros = (0,) * len(shape)
    return pl.BlockSpec(shape, lambda *_: zeros, pipeline_mode=pl.Buffered(1))


def _params(*sem):
    return pltpu.CompilerParams(dimension_semantics=sem, vmem_limit_bytes=VMEM_LIMIT)


def _softplus_neg_abs(z):
    return jnp.log1p(jnp.exp(-jnp.abs(z)))


def _inproj_kernel(x_ref, w_ref, qsb_ref, k_ref, v_ref, kb_ref, vb_ref, lx_ref, lg_ref,
                   dq_ref, dkv_ref, kr_ref, krr_ref, gates_ref):
    xb = x_ref[...].astype(BF16)

    def mm(a, b):
        return jnp.dot(xb, w_ref[:, a:b], preferred_element_type=F32)

    for h in range(SB_HEADS):
        qsb_ref[h] = mm(C_Q + LANE * h, C_Q + LANE * (h + 1)).astype(BF16)
    k = mm(C_K, C_V)
    k_ref[...] = k
    kb_ref[...] = k.astype(BF16)
    v = mm(C_V, C_LX)
    v_ref[...] = v
    vb_ref[...] = v.astype(BF16)
    lx_ref[...] = mm(C_LX, C_LG)
    lg_ref[...] = mm(C_LG, C_DQ)
    dq_ref[...] = mm(C_DQ, C_DKV)
    dkv_ref[...] = mm(C_DKV, C_KR)
    kr_ref[...] = mm(C_KR, C_KRR)
    krr_ref[...] = mm(C_KRR, C_G)
    d = gates_ref.shape[1] // N_BRANCH
    for i in range(N_BRANCH):
        gates_ref[:, i * d:(i + 1) * d] = mm(C_G + i * d, C_G + (i + 1) * d)


def _inproj(x, w):
    n, d = x.shape
    cols = w.shape[1]
    row = lambda width: pl.BlockSpec((TM, width), lambda i: (i, 0))
    sds = lambda width, dt: jax.ShapeDtypeStruct((n, width), dt)
    return pl.pallas_call(
        _inproj_kernel,
        grid=(n // TM,),
        in_specs=[row(d), _const_spec((d, cols))],
        out_specs=[pl.BlockSpec((SB_HEADS, TM, LANE), lambda i: (0, i, 0)),
                   row(LANE), row(LANE), row(LANE), row(LANE), row(LRU_WIDTH), row(LRU_WIDTH),
                   row(Q_LORA), row(KV_LORA), row(LANE), row(LANE), row(N_BRANCH * d)],
        out_shape=[jax.ShapeDtypeStruct((SB_HEADS, n, LANE), BF16),
                   sds(LANE, F32), sds(LANE, F32), sds(LANE, BF16), sds(LANE, BF16),
                   sds(LRU_WIDTH, F32), sds(LRU_WIDTH, F32), sds(Q_LORA, F32), sds(KV_LORA, F32),
                   sds(LANE, F32), sds(LANE, F32), sds(N_BRANCH * d, F32)],
        compiler_params=_params("parallel"),
        name="inproj",
    )(x, w)


def _rms(x, g):
    return x * lax.rsqrt(jnp.mean(jnp.square(x), axis=-1, keepdims=True) + RMS_EPS) * g


def _mlaproj_kernel(dq_ref, dkv_ref, kr_ref, krr_ref, pos_ref, inv_ref, qg_ref, kvg_ref, wuq_ref, wuk_ref,
                    qcat_ref, ckv_ref, krope_ref, kcat_ref):
    nope_w = MLA_HEADS * QK_NOPE
    rope_w = MLA_HEADS * LANE
    qn = _rms(dq_ref[...], qg_ref[...]).astype(BF16)
    q_nope = jnp.dot(qn, wuq_ref[:, :nope_w], preferred_element_type=F32)
    q_r = jnp.dot(qn, wuq_ref[:, nope_w:nope_w + rope_w], preferred_element_type=F32)
    q_rr = jnp.dot(qn, wuq_ref[:, nope_w + rope_w:], preferred_element_type=F32)
    ang = pos_ref[...] * inv_ref[...]
    cos = jnp.cos(ang)
    sin = jnp.sin(ang)
    q_lat = jnp.dot(q_nope.astype(BF16), wuk_ref[...], preferred_element_type=F32)
    for h in range(MLA_HEADS):
        sl = slice(h * LANE, (h + 1) * LANE)
        qcat_ref[h, :, :LANE] = (q_lat[:, sl] * MLA_Q_SCALE).astype(BF16)
        qcat_ref[h, :, LANE:] = ((q_r[:, sl] * cos + q_rr[:, sl] * sin) * MLA_Q_SCALE).astype(BF16)
    ckv = _rms(dkv_ref[...], kvg_ref[...])
    krope = kr_ref[...] * cos + krr_ref[...] * sin
    ckv_ref[...] = ckv
    krope_ref[...] = krope
    kcat_ref[:, :LANE] = ckv.astype(BF16)
    kcat_ref[:, LANE:] = krope.astype(BF16)


def _mlaproj(dq, dkv, kr, krr, pos, inv, qg, kvg, wuq, wuk):
    n = dq.shape[0]
    row = lambda width: pl.BlockSpec((TM, width), lambda i: (i, 0))
    return pl.pallas_call(
        _mlaproj_kernel,
        grid=(n // TM,),
        in_specs=[row(Q_LORA), row(KV_LORA), row(LANE), row(LANE), row(1), _const_spec((1, LANE)),
                  _const_spec((1, Q_LORA)), _const_spec((1, KV_LORA)),
                  _const_spec(wuq.shape), _const_spec(wuk.shape)],
        out_specs=[pl.BlockSpec((MLA_HEADS, TM, 2 * LANE), lambda i: (0, i, 0)),
                   row(KV_LORA), row(LANE), row(2 * LANE)],
        out_shape=[jax.ShapeDtypeStruct((MLA_HEADS, n, 2 * LANE), BF16),
                   jax.ShapeDtypeStruct((n, KV_LORA), F32), jax.ShapeDtypeStruct((n, LANE), F32),
                   jax.ShapeDtypeStruct((n, 2 * LANE), BF16)],
        compiler_params=_params("parallel"),
        name="mlaproj",
    )(dq, dkv, kr, krr, pos, inv, qg, kvg, wuq, wuk)


def _sb_block(z, mask, carry, tri):
    c = z.shape[1]
    t = _softplus_neg_abs(z)
    log_beta = jnp.minimum(z, 0.0) - t
    log_keep = jnp.minimum(-z, 0.0) - t
    if mask is not None:
        log_keep = jnp.where(mask, log_keep, 0.0)
    hi = log_keep.astype(BF16)
    lo = (log_keep - hi.astype(F32)).astype(BF16)
    s = jnp.dot(hi, tri, preferred_element_type=F32) + jnp.dot(lo, tri, preferred_element_type=F32)
    suffix = s[:, :c]
    total = s[:, c:]
    reps = c // LANE
    carry_b = carry if reps == 1 else jnp.concatenate([carry] * reps, axis=1)
    w = jnp.exp(log_beta + suffix + carry_b)
    if mask is not None:
        w = jnp.where(mask, w, 0.0)
    return w, total


def _tri(c):
    r = np.arange(c)
    upper = (r[:, None] > r[None, :]).astype(np.float32)
    return jnp.asarray(np.concatenate([upper, np.ones((c, LANE), np.float32)], axis=1), BF16)


def _sbp_kernel(q_ref, k_ref, v_ref, tri_ref, o_ref, carry_ref, acc_ref):
    i = pl.program_id(1)
    rows = SB_HEADS * QB
    tri = tri_ref[...]
    q = q_ref[...].reshape(rows, LANE)

    def step(j, mask):
        start = pl.multiple_of(j * QB, QB)
        kb = k_ref[pl.ds(start, QB), :]
        vb = v_ref[pl.ds(start, QB), :]
        z = lax.dot_general(q, kb, (((1,), (1,)), ((), ())), preferred_element_type=F32)
        w, total = _sb_block(z, mask, carry_ref[...], tri)
        acc_ref[...] += jnp.dot(w.astype(BF16), vb, preferred_element_type=F32)
        carry_ref[...] += total

    carry_ref[...] = jnp.zeros_like(carry_ref)
    acc_ref[...] = jnp.zeros_like(acc_ref)
    qpos = lax.broadcasted_iota(jnp.int32, (rows, QB), 0) % QB
    kpos = lax.broadcasted_iota(jnp.int32, (rows, QB), 1)
    step(i, kpos < qpos)

    def cond(c):
        j, live = c
        return jnp.logical_and(j >= 0, live > SB_LOG_FLOOR)

    def body(c):
        j, _ = c
        step(j, None)
        return j - 1, jnp.max(carry_ref[...])

    lax.while_loop(cond, body, (i - 1, jnp.max(carry_ref[...])))
    for h in range(SB_HEADS):
        o_ref[:, h * LANE:(h + 1) * LANE] = acc_ref[h * QB:(h + 1) * QB, :].astype(BF16)


def _sb_prompt(qsb, kb, vb, n_batch, t_pad):
    n = n_batch * t_pad
    nq = t_pad // QB
    rows = SB_HEADS * QB
    return pl.pallas_call(
        _sbp_kernel,
        grid=(n_batch, nq),
        in_specs=[pl.BlockSpec((SB_HEADS, QB, LANE), lambda b, i: (0, b * nq + i, 0)),
                  pl.BlockSpec((t_pad, LANE), lambda b, i: (b, 0)),
                  pl.BlockSpec((t_pad, LANE), lambda b, i: (b, 0)),
                  _const_spec((QB, QB + LANE))],
        out_specs=pl.BlockSpec((QB, SB_HEADS * LANE), lambda b, i: (b * nq + i, 0)),
        out_shape=jax.ShapeDtypeStruct((n, SB_HEADS * LANE), BF16),
        scratch_shapes=[pltpu.VMEM((rows, LANE), F32), pltpu.VMEM((rows, LANE), F32)],
        compiler_params=_params("parallel", "arbitrary"),
        name="sb_prompt",
    )(qsb, kb, vb, _tri(QB))


def _mlap_kernel(q_ref, kc_ref, wuv_ref, o_ref, m_ref, l_ref, acc_ref):
    i = pl.program_id(1)
    rows = MLA_HEADS * QB
    m_ref[...] = jnp.full_like(m_ref, NEG_INF)
    l_ref[...] = jnp.zeros_like(l_ref)
    acc_ref[...] = jnp.zeros_like(acc_ref)
    q = q_ref[...].reshape(rows, 2 * LANE)

    def keys(j):
        return kc_ref[pl.ds(pl.multiple_of(j * MLA_KB, MLA_KB), MLA_KB), :]

    def scores(j):
        return lax.dot_general(q, keys(j), (((1,), (1,)), ((), ())), preferred_element_type=F32)

    def accumulate(j, s):
        parts = [s[:, c * LANE:(c + 1) * LANE] for c in range(MLA_KB // LANE)]
        m_old = m_ref[...]
        m_new = jnp.maximum(m_old, jnp.max(functools.reduce(jnp.maximum, parts), axis=1, keepdims=True))
        alpha = jnp.exp2(m_old - m_new)
        probs = [jnp.exp2(part - m_new) for part in parts]
        l_ref[...] = alpha * l_ref[...] + functools.reduce(jnp.add, probs)
        p = jnp.concatenate([pr.astype(BF16) for pr in probs], axis=1)
        acc_ref[...] = alpha * acc_ref[...] + jnp.dot(p, keys(j)[:, :LANE], preferred_element_type=F32)
        m_ref[...] = m_new

    def body(j, s):
        s_next = scores(j + 1)
        accumulate(j, s)
        return s_next

    last = (i * QB) // MLA_KB
    s = lax.fori_loop(0, last, body, scores(0))
    qpos = i * QB + lax.broadcasted_iota(jnp.int32, (rows, MLA_KB), 0) % QB
    kpos = last * MLA_KB + lax.broadcasted_iota(jnp.int32, (rows, MLA_KB), 1)
    accumulate(last, jnp.where(kpos <= qpos, s, NEG_INF))
    o_lat = (acc_ref[...] / jnp.sum(l_ref[...], axis=1, keepdims=True)).astype(BF16)
    out = jnp.zeros((QB, MLA_HEADS * V_HEAD), F32)
    for h in range(MLA_HEADS):
        out = out + jnp.dot(o_lat[h * QB:(h + 1) * QB], wuv_ref[h], preferred_element_type=F32)
    o_ref[...] = out.astype(BF16)


def _mla_prompt(qcat, kcat, wuv, n_batch, t_pad):
    n = n_batch * t_pad
    nq = t_pad // QB
    rows = MLA_HEADS * QB
    return pl.pallas_call(
        _mlap_kernel,
        grid=(n_batch, nq),
        in_specs=[pl.BlockSpec((MLA_HEADS, QB, 2 * LANE), lambda b, i: (0, b * nq + i, 0)),
                  pl.BlockSpec((t_pad, 2 * LANE), lambda b, i: (b, 0)),
                  _const_spec(wuv.shape)],
        out_specs=pl.BlockSpec((QB, MLA_HEADS * V_HEAD), lambda b, i: (b * nq + i, 0)),
        out_shape=jax.ShapeDtypeStruct((n, MLA_HEADS * V_HEAD), BF16),
        scratch_shapes=[pltpu.VMEM((rows, LANE), F32)] * 3,
        compiler_params=_params("parallel", "arbitrary"),
        name="mla_prompt",
    )(qcat, kcat, wuv)


def _gelu_tanh(x):
    return 0.5 * x * (1.0 + jnp.tanh(math.sqrt(2.0 / math.pi) * (x + 0.044715 * (x * x * x))))


def _lru_coeffs(xc, wg_ref, brg_ref, big_ref, lam_ref, first_is_start):
    w = xc.shape[1]
    gates = jnp.dot(xc.astype(BF16), wg_ref[...], preferred_element_type=F32)
    r = jax.nn.sigmoid(gates[:, :w] + brg_ref[...])
    i = jax.nn.sigmoid(gates[:, w:] + big_ref[...])
    lam = lam_ref[...]
    softplus_neg_lam = jnp.maximum(-lam, 0.0) + _softplus_neg_abs(lam)
    log_a = -LRU_C * r * softplus_neg_lam
    a = jnp.exp(log_a)
    mult = jnp.sqrt(-jnp.tanh(log_a) * (a * a + 1.0))
    if first_is_start is not None:
        mult = jnp.where(first_is_start, 1.0, mult)
    return a, mult * i * xc


def _lrup_kernel(x_ref, g_ref, cw_ref, cb_ref, wg_ref, brg_ref, big_ref, lam_ref, y_ref, hl_ref,
                 xbuf_ref, h_ref, *, last_row):
    c = pl.program_id(1)
    tc = x_ref.shape[0]
    tail = CONV_WIDTH - 1

    @pl.when(c == 0)
    def _():
        xbuf_ref[:SUBLANE] = jnp.zeros((SUBLANE, xbuf_ref.shape[1]), F32)
        h_ref[...] = jnp.zeros_like(h_ref)

    x = x_ref[...]
    xbuf_ref[SUBLANE:] = x
    xc = cb_ref[...] + xbuf_ref[SUBLANE - tail:SUBLANE - tail + tc] * cw_ref[0:1]
    for j in range(1, CONV_WIDTH):
        xc = xc + xbuf_ref[SUBLANE - tail + j:SUBLANE - tail + j + tc] * cw_ref[j:j + 1]
    xbuf_ref[:SUBLANE] = x[tc - SUBLANE:]

    row = lax.broadcasted_iota(jnp.int32, (tc, 1), 0)
    a, b = _lru_coeffs(xc, wg_ref, brg_ref, big_ref, lam_ref, jnp.logical_and(row == 0, c == 0))
    shift = 1
    while shift < tc:
        a_prev = pltpu.roll(a, shift, 0)
        b_prev = pltpu.roll(b, shift, 0)
        ok = row >= shift
        b = jnp.where(ok, a * b_prev + b, b)
        a = jnp.where(ok, a * a_prev, a)
        shift *= 2
    h = a * h_ref[SUBLANE - 1:SUBLANE] + b
    h_ref[...] = h[tc - SUBLANE:]
    y_ref[...] = (h * _gelu_tanh(g_ref[...])).astype(BF16)

    lc, lr = divmod(last_row, tc)
    base = (lr // SUBLANE) * SUBLANE

    @pl.when(c == lc)
    def _():
        hl_ref[0] = h[base:base + SUBLANE]


def _lru_prompt(lx, lg, cw, cb, wg, brg, big, lam, n_batch, t_pad, t_real):
    n, w = n_batch * t_pad, lx.shape[1]
    nc = t_pad // LRU_CHUNK
    row = pl.BlockSpec((LRU_CHUNK, w), lambda b, c: (b * nc + c, 0))
    y, hl = pl.pallas_call(
        functools.partial(_lrup_kernel, last_row=t_real - 1),
        grid=(n_batch, nc),
        in_specs=[row, row, _const_spec(cw.shape), _const_spec(cb.shape), _const_spec(wg.shape),
                  _const_spec(brg.shape), _const_spec(big.shape), _const_spec(lam.shape)],
        out_specs=[row, pl.BlockSpec((1, SUBLANE, w), lambda b, c: (b, 0, 0))],
        out_shape=[jax.ShapeDtypeStruct((n, w), BF16), jax.ShapeDtypeStruct((n_batch, SUBLANE, w), F32)],
        scratch_shapes=[pltpu.VMEM((LRU_CHUNK + SUBLANE, w), F32), pltpu.VMEM((SUBLANE, w), F32)],
        compiler_params=_params("parallel", "arbitrary"),
        name="lru_prompt",
    )(lx, lg, cw, cb, wg, brg, big, lam)
    return y, hl[:, (t_real - 1) % SUBLANE]


def _lrus_kernel(x_ref, g_ref, conv0_ref, h0_ref, cw_ref, cb_ref, wg_ref, brg_ref, big_ref, lam_ref,
                 y_ref, hn_ref, *, first_pos):
    steps = x_ref.shape[0]
    tail = CONV_WIDTH - 1
    xin = [conv0_ref[j] for j in range(tail)] + [x_ref[t] for t in range(steps)]
    h = h0_ref[...]
    for t in range(steps):
        xc = cb_ref[...] + xin[t] * cw_ref[0:1]
        for j in range(1, CONV_WIDTH):
            xc = xc + xin[t + j] * cw_ref[j:j + 1]
        start = None
        if first_pos + t == 0:
            start = jnp.full((xc.shape[0], 1), True)
        a, b = _lru_coeffs(xc, wg_ref, brg_ref, big_ref, lam_ref, start)
        h = a * h + b
        y_ref[t] = (h * _gelu_tanh(g_ref[t])).astype(BF16)
    hn_ref[...] = h


def _lru_sample(lx_t, lg_t, conv0_t, h0, cw, cb, wg, brg, big, lam, first_pos):
    steps, nb, w = lx_t.shape
    args = (lx_t, lg_t, conv0_t, h0, cw, cb, wg, brg, big, lam)
    return pl.pallas_call(
        functools.partial(_lrus_kernel, first_pos=first_pos),
        grid=(1,),
        in_specs=[_const_spec(a.shape) for a in args],
        out_specs=[pl.BlockSpec((steps, nb, w), lambda i: (0, 0, 0)), pl.BlockSpec((nb, w), lambda i: (0, 0))],
        out_shape=[jax.ShapeDtypeStruct((steps, nb, w), BF16), jax.ShapeDtypeStruct((nb, w), F32)],
        compiler_params=_params("arbitrary"),
        name="lru_sample",
    )(*args)


def _layer_norm(x, g, b):
    mu = jnp.mean(x, axis=-1, keepdims=True)
    xc = x - mu
    var = jnp.mean(jnp.square(xc), axis=-1, keepdims=True)
    return xc * lax.rsqrt(var + LN_EPS) * g + b


def _route(sel, s):
    e_idx = lax.broadcasted_iota(jnp.int32, sel.shape, 0)
    rows = [sel[e:e + 1] for e in range(N_EXPERTS)]
    best_score = None
    best = None
    for g in range(N_GROUPS):
        v = rows[g * EXPERTS_PER_GROUP:(g + 1) * EXPERTS_PER_GROUP]
        top2 = None
        for a in range(EXPERTS_PER_GROUP):
            for b in range(a + 1, EXPERTS_PER_GROUP):
                pair = v[a] + v[b]
                top2 = pair if top2 is None else jnp.maximum(top2, pair)
        if g == 0:
            best_score, best = top2, jnp.zeros(top2.shape, jnp.int32)
        else:
            better = top2 > best_score
            best = jnp.where(better, g, best)
            best_score = jnp.where(better, top2, best_score)
    masked = jnp.where(e_idx // EXPERTS_PER_GROUP == best, sel, NEG_INF)
    m1 = jnp.max(masked, axis=0, keepdims=True)
    i1 = jnp.min(jnp.where(masked == m1, e_idx, N_EXPERTS), axis=0, keepdims=True)
    rest = jnp.where(e_idx == i1, -jnp.inf, masked)
    m2 = jnp.max(rest, axis=0, keepdims=True)
    i2 = jnp.min(jnp.where(rest == m2, e_idx, N_EXPERTS), axis=0, keepdims=True)
    pick1 = e_idx == i1
    pick2 = e_idx == i2
    w1 = jnp.sum(jnp.where(pick1, s, 0.0), axis=0, keepdims=True)
    w2 = jnp.sum(jnp.where(pick2, s, 0.0), axis=0, keepdims=True)
    denom = w1 + w2
    return jnp.where(pick1, w1 / denom, 0.0) + jnp.where(pick2, w2 / denom, 0.0)


def _merge_kernel(x_ref, osb_p_ref, olru_p_ref, omla_p_ref, osb_s_ref, olru_s_ref, omla_s_ref, gates_ref, bg_ref,
                  wb0_ref, wb1_ref, wb2_ref, wo_ref, g1_ref, b1_ref, wr_ref, rb_ref, x1_ref, gate_ref,
                  *, alpha, prompt_tiles):
    d = x_ref.shape[1]
    is_prompt = pl.program_id(0) < prompt_tiles
    branches = ((osb_p_ref, osb_s_ref, wb0_ref), (olru_p_ref, olru_s_ref, wb1_ref), (omla_p_ref, omla_s_ref, wb2_ref))
    mixed = None
    for i, (op_ref, os_ref, w_ref) in enumerate(branches):
        sl = slice(i * d, (i + 1) * d)
        g = jax.nn.sigmoid(gates_ref[:, sl] + bg_ref[:, sl])
        o = jnp.where(is_prompt, op_ref[...], os_ref[...])
        term = g * jnp.dot(o, w_ref[...], preferred_element_type=F32)
        mixed = term if mixed is None else mixed + term
    mix = jnp.dot(mixed.astype(BF16), wo_ref[...], preferred_element_type=F32)
    x1 = _layer_norm(alpha * x_ref[...] + mix, g1_ref[...], b1_ref[...])
    x1_ref[...] = x1
    logits = lax.dot_general(wr_ref[...], x1, (((1,), (1,)), ((), ())),
                             precision=lax.Precision.HIGHEST, preferred_element_type=F32)
    s = jax.nn.sigmoid(logits)
    gate_ref[...] = _route(s + rb_ref[...], s)


def _merge(x, prompt_outs, sample_outs, gates, bg, wb0, wb1, wb2, wo, g1, b1, wr, rb, alpha):
    n, d = x.shape
    prompt_tiles = prompt_outs[0].shape[0] // TM
    assert all(o.shape[0] == prompt_tiles * TM for o in prompt_outs)
    assert all(o.shape[0] == n - prompt_tiles * TM for o in sample_outs)
    row = lambda width: pl.BlockSpec((TM, width), lambda i: (i, 0))
    prow = lambda width: pl.BlockSpec((TM, width), lambda i: (jnp.minimum(i, prompt_tiles - 1), 0))
    srow = lambda width: pl.BlockSpec((TM, width), lambda i: (jnp.maximum(i - prompt_tiles, 0), 0))
    consts = (bg, wb0, wb1, wb2, wo, g1, b1, wr, rb)
    return pl.pallas_call(
        functools.partial(_merge_kernel, alpha=alpha, prompt_tiles=prompt_tiles),
        grid=(n // TM,),
        in_specs=[row(d)] + [prow(o.shape[1]) for o in prompt_outs] + [srow(o.shape[1]) for o in sample_outs]
        + [row(gates.shape[1])] + [_const_spec(c.shape) for c in consts],
        out_specs=[row(d), pl.BlockSpec((N_EXPERTS, TM), lambda i: (0, i))],
        out_shape=[jax.ShapeDtypeStruct((n, d), F32), jax.ShapeDtypeStruct((N_EXPERTS, n), F32)],
        compiler_params=_params("parallel"),
        name="merge",
    )(x, *prompt_outs, *sample_outs, gates, *consts)


def _moe_kernel(x_ref, gate_ref, wg_ref, wu_ref, wd_ref, g2_ref, b2_ref, o_ref, xb_ref, acc_ref, *, alpha):
    e = pl.program_id(1)

    @pl.when(e == 0)
    def _():
        xb_ref[...] = x_ref[...].astype(BF16)
        acc_ref[...] = jnp.zeros_like(acc_ref)

    xb = xb_ref[...]
    gate = gate_ref[...]
    lane = lax.broadcasted_iota(jnp.int32, gate.shape, 1)
    hidden = []
    for u in range(MOE_EXPERTS_PER_STEP):
        hg = jnp.dot(xb, wg_ref[0, u], preferred_element_type=F32)
        hu = jnp.dot(xb, wu_ref[0, u], preferred_element_type=F32)
        gcol = jnp.sum(jnp.where(lane == e * MOE_EXPERTS_PER_STEP + u, gate, 0.0), axis=1, keepdims=True)
        hidden.append(((hg * jax.nn.sigmoid(hg)) * hu * gcol).astype(BF16))
    wd = wd_ref[0].reshape(MOE_EXPERTS_PER_STEP * wd_ref.shape[2], wd_ref.shape[3])
    acc_ref[...] += jnp.dot(jnp.concatenate(hidden, axis=1), wd, preferred_element_type=F32)

    @pl.when(e == pl.num_programs(1) - 1)
    def _():
        o_ref[...] = _layer_norm(alpha * x_ref[...] + acc_ref[...], g2_ref[...], b2_ref[...])


def _moe(x, gate, wg, wu, wd, layer, g2, b2, alpha):
    n, d = x.shape
    f = wg.shape[-1]
    tm = next(t for t in TM_MOE_CHOICES if n % t == 0)
    return pl.pallas_call(
        functools.partial(_moe_kernel, alpha=alpha),
        grid=(n // tm, N_EXPERTS // MOE_EXPERTS_PER_STEP),
        in_specs=[pl.BlockSpec((tm, d), lambda i, e: (i, 0)),
                  pl.BlockSpec((tm, N_EXPERTS), lambda i, e: (i, 0)),
                  pl.BlockSpec((1, MOE_EXPERTS_PER_STEP, d, f), lambda i, e: (layer, e, 0, 0)),
                  pl.BlockSpec((1, MOE_EXPERTS_PER_STEP, d, f), lambda i, e: (layer, e, 0, 0)),
                  pl.BlockSpec((1, MOE_EXPERTS_PER_STEP, f, d), lambda i, e: (layer, e, 0, 0)),
                  _const_spec(g2.shape), _const_spec(b2.shape)],
        out_specs=pl.BlockSpec((tm, d), lambda i, e: (i, 0)),
        out_shape=jax.ShapeDtypeStruct((n, d), F32),
        scratch_shapes=[pltpu.VMEM((tm, d), BF16), pltpu.VMEM((tm, d), F32)],
        compiler_params=_params("parallel", "arbitrary"),
        name="moe",
    )(x, gate, wg, wu, wd, g2, b2)


def _seq_copies(pt_ref, seq, slot, layer, n_pages, page, kc_ref, vc_ref, cc_ref, rc_ref,
                ktop_ref, vtop_ref, cb_ref, rt_ref, sem_ref):
    copies = []
    for j in range(n_pages):
        pid = pt_ref[seq, j]
        copies.append(pltpu.make_async_copy(cc_ref.at[layer, pid], cb_ref.at[slot, pl.ds(j * page, page), :],
                                            sem_ref.at[slot, 0]))
        copies.append(pltpu.make_async_copy(rc_ref.at[layer, pid], rt_ref.at[slot, :, pl.ds(j * page, page)],
                                            sem_ref.at[slot, 1]))
    per_chunk = SB_CHUNK // page
    for u in range(per_chunk):
        pid = pt_ref[seq, n_pages - per_chunk + u]
        lanes = pl.ds(u * page, page)
        copies.append(pltpu.make_async_copy(kc_ref.at[layer, pid], ktop_ref.at[slot, :, :, lanes], sem_ref.at[slot, 2]))
        copies.append(pltpu.make_async_copy(vc_ref.at[layer, pid], vtop_ref.at[slot, :, :, lanes], sem_ref.at[slot, 3]))
    return copies


def _chunk_copies(pt_ref, seq, chunk, par, layer, page, kc_ref, vc_ref, kold_ref, vold_ref, sem_ref):
    copies = []
    per_chunk = SB_CHUNK // page
    for u in range(per_chunk):
        pid = pt_ref[seq, chunk * per_chunk + u]
        lanes = pl.ds(u * page, page)
        copies.append(pltpu.make_async_copy(kc_ref.at[layer, pid], kold_ref.at[par, :, :, lanes], sem_ref.at[par]))
        copies.append(pltpu.make_async_copy(vc_ref.at[layer, pid], vold_ref.at[par, :, :, lanes], sem_ref.at[par]))
    return copies


def _samp_kernel(pt_ref, qsb_ref, knew_ref, vnew_ref, qcat_ref, kcnew_ref, tri_ref, trin_ref, wuv_ref,
                 kc_ref, vc_ref, cc_ref, rc_ref, osb_ref, omla_ref,
                 ktop_ref, vtop_ref, cb_ref, rt_ref, kold_ref, vold_ref, sem_ref, osem_ref, carry_ref, acc_ref,
                 *, layer, n_pages, page, steps):
    s = pl.program_id(0)
    ns = pl.num_programs(0)
    slot = s % 2
    past = n_pages * page
    ahead = (ktop_ref, vtop_ref, cb_ref, rt_ref, sem_ref)
    caches = (kc_ref, vc_ref, cc_ref, rc_ref)

    @pl.when(s == 0)
    def _():
        for cp in _seq_copies(pt_ref, 0, 0, layer, n_pages, page, *caches, *ahead):
            cp.start()

    @pl.when(s + 1 < ns)
    def _():
        for cp in _seq_copies(pt_ref, s + 1, 1 - slot, layer, n_pages, page, *caches, *ahead):
            cp.start()

    for cp in _seq_copies(pt_ref, s, slot, layer, n_pages, page, *caches, *ahead):
        cp.wait()

    rows = SB_GROUP * SUBLANE
    tq = lax.broadcasted_iota(jnp.int32, (rows, LANE), 0) % SUBLANE
    tk = lax.broadcasted_iota(jnp.int32, (rows, LANE), 1)
    new_mask = tk < tq
    tri = tri_ref[...]
    n_chunks = past // SB_CHUNK

    def past_chunk(kv, qk, k_ref, v_ref, buf):
        kt = k_ref[buf, kv].astype(BF16)
        vt = v_ref[buf, kv].astype(BF16)
        z = jnp.dot(qk, kt, preferred_element_type=F32)
        w, total = _sb_block(z, None, carry_ref[kv], tri)
        pv = lax.dot_general(w.astype(BF16), vt, (((1,), (1,)), ((), ())), preferred_element_type=F32)
        acc_ref[kv, :, kv * SB_HEAD_DIM:(kv + 1) * SB_HEAD_DIM] += pv
        carry_ref[kv] += total

    qks = []
    for kv in range(SB_KV_HEADS):
        q = qsb_ref[0, kv * SB_GROUP:(kv + 1) * SB_GROUP].reshape(rows, LANE).astype(BF16)
        z = lax.dot_general(q, knew_ref[0], (((1,), (1,)), ((), ())), preferred_element_type=F32)
        w, total = _sb_block(z, new_mask, jnp.zeros((rows, LANE), F32), trin_ref[...])
        acc_ref[kv] = jnp.dot(w.astype(BF16), vnew_ref[0], preferred_element_type=F32)
        carry_ref[kv] = total
        qks.append(q[:, kv * SB_HEAD_DIM:(kv + 1) * SB_HEAD_DIM])
        past_chunk(kv, qks[kv], ktop_ref, vtop_ref, slot)

    mrows = MLA_HEADS * SUBLANE
    q = qcat_ref[0].reshape(mrows, 2 * LANE).astype(BF16)
    cb = cb_ref[slot].astype(BF16)
    rt = rt_ref[slot].astype(BF16)
    s_past = lax.dot_general(q[:, :LANE], cb, (((1,), (1,)), ((), ())), preferred_element_type=F32)
    s_past = s_past + jnp.dot(q[:, LANE:LANE + QK_ROPE], rt, preferred_element_type=F32)
    kcn = kcnew_ref[0]
    s_new = lax.dot_general(q, kcn, (((1,), (1,)), ((), ())), preferred_element_type=F32)
    tq = lax.broadcasted_iota(jnp.int32, (mrows, LANE), 0) % SUBLANE
    tk = lax.broadcasted_iota(jnp.int32, (mrows, LANE), 1)
    s_new = jnp.where(jnp.logical_and(tk <= tq, tk < steps), s_new, NEG_INF)
    m = jnp.maximum(jnp.max(s_past, axis=1, keepdims=True), jnp.max(s_new, axis=1, keepdims=True))
    p_past = jnp.exp2(s_past - m)
    p_new = jnp.exp2(s_new - m)
    denom = jnp.sum(p_past, axis=1, keepdims=True) + jnp.sum(p_new, axis=1, keepdims=True)
    o_lat = jnp.dot(p_past.astype(BF16), cb, preferred_element_type=F32)
    o_lat = o_lat + jnp.dot(p_new.astype(BF16), kcn[:, :LANE], preferred_element_type=F32)
    o_lat = o_lat / denom
    out = jnp.zeros((SUBLANE, MLA_HEADS * V_HEAD), F32)
    for h in range(MLA_HEADS):
        out = out + jnp.dot(o_lat[h * SUBLANE:(h + 1) * SUBLANE].astype(BF16), wuv_ref[h],
                            preferred_element_type=F32)
    omla_ref[0] = out

    if n_chunks >= 2:
        old = (kc_ref, vc_ref, kold_ref, vold_ref, osem_ref)
        first = n_chunks - 2
        live0 = jnp.max(carry_ref[...])

        @pl.when(live0 > SB_LOG_FLOOR)
        def _():
            for cp in _chunk_copies(pt_ref, s, first, first % 2, layer, page, *old):
                cp.start()

        def cond(c):
            j, live = c
            return jnp.logical_and(j >= 0, live > SB_LOG_FLOOR)

        def body(c):
            j, _ = c
            par = j % 2
            for cp in _chunk_copies(pt_ref, s, j, par, layer, page, *old):
                cp.wait()

            @pl.when(j >= 1)
            def _():
                for cp in _chunk_copies(pt_ref, s, j - 1, 1 - par, layer, page, *old):
                    cp.start()

            for kv in range(SB_KV_HEADS):
                past_chunk(kv, qks[kv], kold_ref, vold_ref, par)
            return j - 1, jnp.max(carry_ref[...])

        j_end, _ = lax.while_loop(cond, body, (first, live0))

        @pl.when(jnp.logical_and(live0 > SB_LOG_FLOOR, j_end >= 0))
        def _():
            for cp in _chunk_copies(pt_ref, s, j_end, j_end % 2, layer, page, *old):
                cp.wait()

    for kv in range(SB_KV_HEADS):
        osb_ref[0, kv] = acc_ref[kv]


def _sample_attn(page_table, qsb_s, knew, vnew, qcat_s, kcnew, wuv, kc, vc, cc, rc, layer, steps):
    nseq, n_pages = page_table.shape
    page = cc.shape[2]
    past = n_pages * page
    assert past % SB_CHUNK == 0 and SB_CHUNK % page == 0
    rows = SB_GROUP * SUBLANE
    any_spec = pl.BlockSpec(memory_space=pl.ANY)
    seq4 = lambda a, b, c: pl.BlockSpec((1, a, b, c), lambda s, pt: (s, 0, 0, 0))
    seq3 = lambda a, b: pl.BlockSpec((1, a, b), lambda s, pt: (s, 0, 0))
    const = lambda shape: pl.BlockSpec(shape, lambda s, pt: (0,) * len(shape))
    grid_spec = pltpu.PrefetchScalarGridSpec(
        num_scalar_prefetch=1,
        grid=(nseq,),
        in_specs=[seq4(SB_HEADS, SUBLANE, LANE), seq3(LANE, LANE), seq3(LANE, LANE),
                  seq4(MLA_HEADS, SUBLANE, 2 * LANE), seq3(LANE, 2 * LANE),
                  const((SB_CHUNK, SB_CHUNK + LANE)), const((LANE, 2 * LANE)), const(wuv.shape),
                  any_spec, any_spec, any_spec, any_spec],
        out_specs=[seq4(SB_KV_HEADS, rows, LANE), seq3(SUBLANE, MLA_HEADS * V_HEAD)],
        scratch_shapes=[pltpu.VMEM((2, SB_KV_HEADS, SB_HEAD_DIM, SB_CHUNK), F32),
                        pltpu.VMEM((2, SB_KV_HEADS, SB_HEAD_DIM, SB_CHUNK), F32),
                        pltpu.VMEM((2, past, KV_LORA), F32),
                        pltpu.VMEM((2, QK_ROPE, past), F32),
                        pltpu.VMEM((2, SB_KV_HEADS, SB_HEAD_DIM, SB_CHUNK), F32),
                        pltpu.VMEM((2, SB_KV_HEADS, SB_HEAD_DIM, SB_CHUNK), F32),
                        pltpu.SemaphoreType.DMA((2, 4)),
                        pltpu.SemaphoreType.DMA((2,)),
                        pltpu.VMEM((SB_KV_HEADS, rows, LANE), F32), pltpu.VMEM((SB_KV_HEADS, rows, LANE), F32)])
    return pl.pallas_call(
        functools.partial(_samp_kernel, layer=layer, n_pages=n_pages, page=page, steps=steps),
        grid_spec=grid_spec,
        out_shape=[jax.ShapeDtypeStruct((nseq, SB_KV_HEADS, rows, LANE), F32),
                   jax.ShapeDtypeStruct((nseq, SUBLANE, MLA_HEADS * V_HEAD), F32)],
        compiler_params=_params("arbitrary"),
        name="sample_attn",
    )(page_table, qsb_s, knew, vnew, qcat_s, kcnew, _tri(SB_CHUNK), _tri(LANE), wuv, kc, vc, cc, rc)


def _pad_last(a, width):
    return jnp.pad(a, [(0, 0)] * (a.ndim - 1) + [(0, width - a.shape[-1])])


def _rot_half(w):
    half = w.shape[-1] // 2
    return jnp.concatenate([-w[..., half:], w[..., :half]], axis=-1)


def _pack_w_in(w_in):
    depth, d, _ = w_in.shape
    widths = (SB_HEADS * SB_HEAD_DIM, SB_KV_HEADS * SB_HEAD_DIM, SB_KV_HEADS * SB_HEAD_DIM, LRU_WIDTH, LRU_WIDTH,
              Q_LORA, KV_LORA, QK_ROPE, N_BRANCH * d)
    offs = np.concatenate([[0], np.cumsum(widths)])
    wq, wk, wv, wlx, wlg, wdq, wdkv, wkr, wgt = [w_in[..., offs[i]:offs[i + 1]] for i in range(len(widths))]
    wq = wq.reshape(depth, d, SB_HEADS, SB_HEAD_DIM) * (SB_HEAD_DIM ** -0.5)
    zero = jnp.zeros_like(wq)
    first_kv = (jnp.arange(SB_HEADS) // SB_GROUP == 0)[None, None, :, None]
    wq = jnp.where(first_kv, jnp.concatenate([wq, zero], -1), jnp.concatenate([zero, wq], -1))
    wq = wq.reshape(depth, d, SB_HEADS * LANE)
    packed = jnp.concatenate([wq, wk, wv, wlx, wlg, wdq, wdkv, _pad_last(wkr, LANE), _pad_last(_rot_half(wkr), LANE), wgt],
                             axis=-1)
    return packed.astype(BF16)


def _pack_w_uq(w_uq):
    depth = w_uq.shape[0]
    w = w_uq.reshape(depth, Q_LORA, MLA_HEADS, QK_NOPE + QK_ROPE)
    nope = w[..., :QK_NOPE].reshape(depth, Q_LORA, MLA_HEADS * QK_NOPE)
    rope = w[..., QK_NOPE:]
    rope_p = _pad_last(rope, LANE).reshape(depth, Q_LORA, MLA_HEADS * LANE)
    rot_p = _pad_last(_rot_half(rope), LANE).reshape(depth, Q_LORA, MLA_HEADS * LANE)
    return jnp.concatenate([nope, rope_p, rot_p], axis=-1).astype(BF16)


def _pack_w_ukv(w_ukv):
    depth = w_ukv.shape[0]
    w = w_ukv.reshape(depth, KV_LORA, MLA_HEADS, QK_NOPE + V_HEAD)
    eye = jnp.eye(MLA_HEADS, dtype=w.dtype)
    w_uk_t = jnp.transpose(w[..., :QK_NOPE], (0, 2, 3, 1))
    wuk_bd = w_uk_t[:, :, :, None, :] * eye[None, :, None, :, None]
    wuk_bd = wuk_bd.reshape(depth, MLA_HEADS * QK_NOPE, MLA_HEADS * KV_LORA)
    w_uv = jnp.transpose(w[..., QK_NOPE:], (0, 2, 1, 3))
    wuv_p = w_uv[:, :, :, None, :] * eye[None, :, None, :, None]
    wuv_p = wuv_p.reshape(depth, MLA_HEADS, KV_LORA, MLA_HEADS * V_HEAD)
    return wuk_bd.astype(BF16), wuv_p.astype(BF16)


def _pack_lru_gates(w_rg, w_ig):
    depth = w_rg.shape[0]
    bd = w_rg.shape[-1]
    eye = jnp.eye(LRU_BLOCKS, dtype=w_rg.dtype)

    def block_diag(w):
        full = w[:, :, :, None, :] * eye[None, :, None, :, None]
        return full.reshape(depth, LRU_BLOCKS * bd, LRU_BLOCKS * bd)

    return jnp.concatenate([block_diag(w_rg), block_diag(w_ig)], axis=-1).astype(BF16)


def _pack_w_branch0(w):
    depth, _, d = w.shape
    w = w.reshape(depth, SB_HEADS, SB_HEAD_DIM, d)
    zero = jnp.zeros_like(w)
    first_kv = (jnp.arange(SB_HEADS) // SB_GROUP == 0)[None, :, None, None]
    w = jnp.where(first_kv, jnp.concatenate([w, zero], 2), jnp.concatenate([zero, w], 2))
    return w.reshape(depth, SB_HEADS * LANE, d).astype(BF16)


@jax.jit
def kernel(x_prompt, x_sample, cache_sb_k, cache_sb_v, cache_mla_ckv, cache_mla_krope, state_lru_h, state_lru_conv,
           page_table, meta_tokens, w_in, b_gate, conv_w, conv_b, w_rg, b_rg, w_ig, b_ig, lru_lambda, q_norm_g, w_uq,
           kv_norm_g, w_ukv, w_branch, w_out, ln1_g, ln1_b, ln2_g, ln2_b, w_router, router_bias, w_exp_gate,
           w_exp_up, w_exp_down):
    nb, seq, d = x_prompt.shape
    n_meta = meta_tokens.shape[0]
    t_real = n_meta + seq
    t_pad = -(-t_real // MLA_KB) * MLA_KB
    db, steps, _ = x_sample.shape
    depth = w_in.shape[0]
    n_pages = page_table.shape[1]
    page = cache_mla_ckv.shape[2]
    past = n_pages * page
    n_prompt = nb * t_pad
    n_samp = db * steps
    n_tok = -(-(n_prompt + n_samp) // ROW_ALIGN) * ROW_ALIGN
    alpha = (2 * depth) ** 0.25
    assert steps <= SUBLANE and t_pad % LRU_CHUNK == 0

    xp = jnp.concatenate([jnp.broadcast_to(meta_tokens[None], (nb, n_meta, d)), x_prompt,
                          jnp.zeros((nb, t_pad - t_real, d), F32)], axis=1).reshape(n_prompt, d)
    x = jnp.concatenate([xp, x_sample.reshape(n_samp, d), jnp.zeros((n_tok - n_prompt - n_samp, d), F32)], axis=0)
    pos_p = jnp.tile(jnp.arange(t_pad, dtype=F32), nb)
    pos_s = jnp.tile(past + jnp.arange(steps, dtype=F32), db)
    pos = jnp.concatenate([pos_p, pos_s, jnp.zeros((n_tok - n_prompt - n_samp,), F32)])[:, None]
    half = QK_ROPE // 2
    inv = ROPE_THETA ** (-jnp.arange(half, dtype=F32) / half)
    inv = jnp.tile(inv, LANE // half)[None]

    w_in_p = _pack_w_in(w_in)
    wuq_p = _pack_w_uq(w_uq)
    wuk_bd, wuv_p = _pack_w_ukv(w_ukv)
    wgates = _pack_lru_gates(w_rg, w_ig)
    wb0 = _pack_w_branch0(w_branch[:, 0])
    wb1 = w_branch[:, 1].astype(BF16)
    wb2 = w_branch[:, 2].astype(BF16)
    wo = w_out.astype(BF16)
    weg = w_exp_gate.astype(BF16)
    weu = w_exp_up.astype(BF16)
    wed = w_exp_down.astype(BF16)
    wr_t = jnp.transpose(w_router)
    rb = router_bias[:, None]
    kc = jnp.transpose(cache_sb_k, (0, 1, 3, 4, 2))
    vc = jnp.transpose(cache_sb_v, (0, 1, 3, 4, 2))
    rc = jnp.transpose(cache_mla_krope, (0, 1, 3, 2))
    conv0_t = jnp.transpose(state_lru_conv, (0, 2, 1, 3))

    def samp_rows(a):
        return a[..., n_prompt:n_prompt + n_samp, :]

    def per_seq_heads(a):
        h, _, c = a.shape
        a = a.reshape(h, db, steps, c)
        a = jnp.pad(a, ((0, 0), (0, 0), (0, SUBLANE - steps), (0, 0)))
        return jnp.transpose(a, (1, 0, 2, 3)).astype(F32)

    def per_seq_keys(a):
        a = a.reshape(db, steps, a.shape[-1])
        return jnp.pad(a, ((0, 0), (0, LANE - steps), (0, 0)))

    st_p = []
    st_s = []
    for l in range(depth):
        (qsb, k, v, kb, vb, lx, lg, dq, dkv, kr, krr, gates) = _inproj(x, w_in_p[l])
        qcat, ckv, krope, kcat = _mlaproj(dq, dkv, kr, krr, pos, inv, q_norm_g[l][None], kv_norm_g[l][None],
                                          wuq_p[l], wuk_bd[l])
        lru_w = (conv_w[l], conv_b[l][None], wgates[l], b_rg[l][None], b_ig[l][None], lru_lambda[l][None])

        osb = _sb_prompt(qsb, kb, vb, nb, t_pad)
        omla = _mla_prompt(qcat, kcat, wuv_p[l], nb, t_pad)
        olru, h_p = _lru_prompt(lx, lg, *lru_w, nb, t_pad, t_real)

        osb_s, omla_s = _sample_attn(page_table, per_seq_heads(samp_rows(qsb)), per_seq_keys(samp_rows(kb)),
                                     per_seq_keys(samp_rows(vb)), per_seq_heads(samp_rows(qcat)),
                                     per_seq_keys(samp_rows(kcat)), wuv_p[l], kc, vc, cache_mla_ckv, rc, l, steps)
        lx_s = samp_rows(lx).reshape(db, steps, LRU_WIDTH)
        lx_t = jnp.transpose(lx_s, (1, 0, 2))
        lg_t = jnp.transpose(samp_rows(lg).reshape(db, steps, LRU_WIDTH), (1, 0, 2))
        olru_t, h_s = _lru_sample(lx_t, lg_t, conv0_t[l], state_lru_h[l], *lru_w, past)

        osb_s = osb_s.reshape(db, SB_KV_HEADS, SB_GROUP, SUBLANE, LANE)[:, :, :, :steps]
        osb_s = jnp.transpose(osb_s, (0, 3, 1, 2, 4)).reshape(n_samp, SB_HEADS * LANE).astype(BF16)
        omla_s = omla_s[:, :steps].reshape(n_samp, MLA_HEADS * V_HEAD).astype(BF16)
        olru_s = jnp.transpose(olru_t, (1, 0, 2)).reshape(n_samp, LRU_WIDTH)
        tail_pad = ((0, n_tok - n_prompt - n_samp), (0, 0))
        sample_outs = [jnp.pad(o, tail_pad) for o in (osb_s, olru_s, omla_s)]

        x1, gate_t = _merge(x, (osb, olru, omla), sample_outs, gates, b_gate[l][None], wb0[l], wb1[l], wb2[l],
                            wo[l], ln1_g[l][None], ln1_b[l][None], wr_t, rb, alpha)
        x = _moe(x1, jnp.transpose(gate_t), weg, weu, wed, l, ln2_g[l][None], ln2_b[l][None], alpha)

        def prompt_state(a, width):
            return a[:n_prompt].reshape(nb, t_pad, -1)[:, :t_real, :width]

        conv_in_s = jnp.concatenate([state_lru_conv[l], lx_s], axis=1)
        st_p.append((prompt_state(k, LANE).reshape(nb, t_real, SB_KV_HEADS, SB_HEAD_DIM),
                     prompt_state(v, LANE).reshape(nb, t_real, SB_KV_HEADS, SB_HEAD_DIM),
                     prompt_state(ckv, KV_LORA), prompt_state(krope, QK_ROPE), h_p,
                     prompt_state(lx, LRU_WIDTH)[:, t_real - (CONV_WIDTH - 1):]))
        st_s.append((samp_rows(k).reshape(db, steps, SB_KV_HEADS, SB_HEAD_DIM),
                     samp_rows(v).reshape(db, steps, SB_KV_HEADS, SB_HEAD_DIM),
                     samp_rows(ckv).reshape(db, steps, KV_LORA),
                     samp_rows(krope)[:, :QK_ROPE].reshape(db, steps, QK_ROPE), h_s,
                     conv_in_s[:, steps:]))

    p_state = [jnp.stack(f, axis=0) for f in zip(*st_p)]
    s_state = [jnp.stack(f, axis=0) for f in zip(*st_s)]
    y_prompt = x[:n_prompt].reshape(nb, t_pad, d)[:, n_meta:t_real]
    y_sample = x[n_prompt:n_prompt + n_samp].reshape(db, steps, d)
    return (y_prompt, y_sample, *p_state, *s_state)
```
